```python
import math
import jax, jax.numpy as jnp
from jax import lax
import numpy as np

D_MODEL = 2048
BATCH = 16
SEQ = 2048
DEPTH = 1
DEC_BATCH = 32
DEC_SEQ = 64
PAST_LEN = 4096

CHUNK = 64
N_META = 16
META_PAD = (-N_META) % CHUNK
CONV_K = 4
EPS = 1e-6

SSD_D_INNER = D_MODEL
SSD_HEAD_DIM = 64
SSD_HEADS = SSD_D_INNER // SSD_HEAD_DIM
SSD_GROUPS = 4
SSD_HPG = SSD_HEADS // SSD_GROUPS
SSD_STATE = 128
SSD_CONV_CH = SSD_D_INNER + 2 * SSD_GROUPS * SSD_STATE

GDN_HEADS = 16
GDN_HEAD_DIM = D_MODEL // GDN_HEADS
GDN_WIDTH = GDN_HEADS * GDN_HEAD_DIM
GDN_CONV_CH = 3 * GDN_WIDTH

MIX_WIDTH = SSD_D_INNER + GDN_WIDTH
D_FF = 4 * D_MODEL

SPLITS = (SSD_D_INNER, SSD_CONV_CH, SSD_HEADS, GDN_CONV_CH, GDN_WIDTH, GDN_HEADS, GDN_HEADS, D_MODEL, D_MODEL)
IN_COLS = sum(SPLITS)
SPLIT_IDX = tuple(int(v) for v in np.cumsum(SPLITS)[:-1])

kernel_name = "hybrid_ssd_gdn_streaming_step"


def _rmsnorm(x, w):
    xf = x.astype(jnp.float32)
    xf = xf * lax.rsqrt(jnp.mean(xf * xf, axis=-1, keepdims=True) + EPS)
    return (xf * w.astype(jnp.float32)).astype(x.dtype)


def _l2norm(x):
    return x * lax.rsqrt(jnp.sum(x * x, axis=-1, keepdims=True) + EPS)


def _pad_time(a, left, right):
    pads = [(0, 0)] * a.ndim
    pads[1] = (left, right)
    return jnp.pad(a, pads)


def _causal_conv(x, buf, w, b):
    xp = jnp.concatenate([buf.astype(x.dtype), x], axis=1)
    y = lax.conv_general_dilated(xp, w[:, None, :].astype(x.dtype), window_strides=(1,), padding='VALID',
                                 dimension_numbers=('NWC', 'WIO', 'NWC'), feature_group_count=x.shape[-1])
    if b is not None:
        y = y + b.astype(x.dtype)
    return y, xp[:, -(CONV_K - 1):]


def _ssd_scan(x, dt, a, bm, cm, h0):
    b, l = x.shape[:2]
    nc = l // CHUNK

    def blocks(t):
        return jnp.moveaxis(t.reshape(b, nc, CHUNK, *t.shape[2:]), 1, 0)

    causal = jnp.tril(jnp.ones((CHUNK, CHUNK), bool))[None, :, :, None, None]

    def step(h, inp):
        xc, dtc, bc, cc = inp
        acs = jnp.cumsum(dtc * a, axis=1)
        lmat = jnp.exp(jnp.where(causal, acs[:, :, None] - acs[:, None, :], -jnp.inf))
        xdt = xc * dtc[..., None]
        cb = jnp.einsum('blgn,bsgn->blsg', cc, bc)
        y_diag = jnp.einsum('blsg,blsgr,bsgrp->blgrp', cb, lmat, xdt)
        y_off = jnp.einsum('blgn,bgrpn->blgrp', cc, h) * jnp.exp(acs)[..., None]
        h_new = (h * jnp.exp(acs[:, -1])[..., None, None]
                 + jnp.einsum('bsgn,bsgr,bsgrp->bgrpn', bc, jnp.exp(acs[:, -1:] - acs), xdt))
        return h_new, y_diag + y_off

    h_fin, y = lax.scan(step, h0, (blocks(x), blocks(dt), blocks(bm), blocks(cm)))
    y = jnp.moveaxis(y, 0, 1).reshape(x.shape)
    return y, h_fin


def _gdn_scan(q, k, v, g, beta, s0):
    b, h, l = g.shape
    nc = l // CHUNK

    def blocks(t):
        return jnp.moveaxis(t.reshape(b, h, nc, CHUNK, *t.shape[3:]), 2, 0)

    incl = jnp.tril(jnp.ones((CHUNK, CHUNK), bool))
    strict = jnp.tril(jnp.ones((CHUNK, CHUNK), bool), -1)
    eye = jnp.eye(CHUNK, dtype=jnp.float32)

    def step(s, inp):
        qc, kc, vc, gc, bc = inp
        gcum = jnp.cumsum(gc, axis=-1)
        decay = jnp.exp(jnp.where(incl, gcum[..., :, None] - gcum[..., None, :], -jnp.inf))
        lmat = jnp.where(strict, bc[..., :, None] * jnp.einsum('bhtk,bhsk->bhts', kc, kc) * decay, 0.0)
        tinv = lax.linalg.triangular_solve(eye + lmat, jnp.broadcast_to(eye, lmat.shape),
                                           left_side=True, lower=True, unit_diagonal=True)
        u = jnp.einsum('bhts,bhsv->bhtv', tinv, vc * bc[..., None])
        w = jnp.einsum('bhts,bhsk->bhtk', tinv, kc * (bc * jnp.exp(gcum))[..., None])
        wv = u - jnp.einsum('bhtk,bhkv->bhtv', w, s)
        qk = jnp.einsum('bhtk,bhsk->bhts', qc, kc) * decay
        o = (jnp.einsum('bhtk,bhkv->bhtv', qc * jnp.exp(gcum)[..., None], s)
             + jnp.einsum('bhts,bhsv->bhtv', qk, wv))
        s_new = (s * jnp.exp(gcum[..., -1])[..., None, None]
                 + jnp.einsum('bhsk,bhsv->bhkv', kc * jnp.exp(gcum[..., -1:] - gcum)[..., None], wv))
        return s_new, o

    s_fin, o = lax.scan(step, s0, (blocks(q), blocks(k), blocks(v), blocks(g), blocks(beta)))
    o = jnp.moveaxis(o, 0, 2).reshape(b, h, l, -1)
    return o, s_fin


def _layer(x, lpad, conv_a, ssm_a, conv_b, ssm_b,
           norm_mix_w, w_in, ssd_conv_w, ssd_conv_b, ssd_dt_bias, ssd_a_log, ssd_d, ssd_norm_w,
           gdn_conv_w, gdn_dt_bias, gdn_a_log, gdn_norm_w, w_out, norm_mlp_w, w_up, w_down):
    f32 = jnp.float32
    b, l, _ = x.shape
    rpad = (-(lpad + l)) % CHUNK

    def pad(t):
        return _pad_time(t, lpad, rpad)

    u = _rmsnorm(x, norm_mix_w)
    z_a, xbc_a, dt_a, qkv_b, z_b, beta_b, a_b, gate_a, gate_b = jnp.split(u @ w_in, SPLIT_IDX, axis=-1)

    xbc, conv_a_new = _causal_conv(xbc_a, conv_a, ssd_conv_w, ssd_conv_b)
    xbc = jax.nn.silu(xbc).astype(f32)
    xs, bm, cm = jnp.split(xbc, [SSD_D_INNER, SSD_D_INNER + SSD_GROUPS * SSD_STATE], axis=-1)
    xs = xs.reshape(b, l, SSD_GROUPS, SSD_HPG, SSD_HEAD_DIM)
    bm = bm.reshape(b, l, SSD_GROUPS, SSD_STATE)
    cm = cm.reshape(b, l, SSD_GROUPS, SSD_STATE)
    dt = jax.nn.softplus(dt_a.astype(f32) + ssd_dt_bias.astype(f32)).reshape(b, l, SSD_GROUPS, SSD_HPG)
    a = -jnp.exp(ssd_a_log.astype(f32)).reshape(SSD_GROUPS, SSD_HPG)
    h0 = ssm_a.astype(f32).reshape(b, SSD_GROUPS, SSD_HPG, SSD_HEAD_DIM, SSD_STATE)
    ya, ssm_a_new = _ssd_scan(pad(xs), pad(dt), a, pad(bm), pad(cm), h0)
    ya = ya[:, lpad:lpad + l] + xs * ssd_d.astype(f32).reshape(SSD_GROUPS, SSD_HPG, 1)
    ya = ya.reshape(b, l, SSD_D_INNER) * jax.nn.silu(z_a.astype(f32))
    ya = _rmsnorm(ya.reshape(b, l, SSD_GROUPS, -1), ssd_norm_w.reshape(SSD_GROUPS, -1)).reshape(b, l, SSD_D_INNER)

    qkv, conv_b_new = _causal_conv(qkv_b, conv_b, gdn_conv_w, None)
    qkv = jax.nn.silu(qkv).astype(f32).reshape(b, l, 3, GDN_HEADS, GDN_HEAD_DIM)
    q = _l2norm(qkv[:, :, 0]) * (GDN_HEAD_DIM ** -0.5)
    k = _l2norm(qkv[:, :, 1])
    v = qkv[:, :, 2]
    beta = jax.nn.sigmoid(beta_b.astype(f32))
    g = -jnp.exp(gdn_a_log.astype(f32)) * jax.nn.softplus(a_b.astype(f32) + gdn_dt_bias.astype(f32))

    def head_major(t):
        return jnp.moveaxis(pad(t), 2, 1)

    ob, ssm_b_new = _gdn_scan(head_major(q), head_major(k), head_major(v), head_major(g),
                              head_major(beta), ssm_b.astype(f32))
    ob = jnp.moveaxis(ob, 1, 2)[:, lpad:lpad + l]
    ob = _rmsnorm(ob, gdn_norm_w) * jax.nn.silu(z_b.astype(f32).reshape(b, l, GDN_HEADS, GDN_HEAD_DIM))
    ob = ob.reshape(b, l, GDN_WIDTH)

    xd = x.dtype
    mix = (jax.nn.sigmoid(gate_a) * (ya.astype(xd) @ w_out[:SSD_D_INNER])
           + jax.nn.sigmoid(gate_b) * (ob.astype(xd) @ w_out[SSD_D_INNER:]))
    h = x + mix

    hid = jnp.square(jax.nn.relu(_rmsnorm(h, norm_mlp_w) @ w_up))
    h = h + hid @ w_down
    return (h, conv_a_new,
            ssm_a_new.reshape(b, SSD_HEADS, SSD_HEAD_DIM, SSD_STATE).astype(ssm_a.dtype),
            conv_b_new, ssm_b_new.astype(ssm_b.dtype))


def setup_inputs(seed: int = 0) -> dict:
    key = jax.random.key(seed)
    ks = iter(jax.random.split(key, 40))
    L = DEPTH

    def nrm(shape, s):
        return jax.random.normal(next(ks), shape, jnp.float32) * s

    def gain(shape):
        return 1.0 + nrm(shape, 0.01)

    def dt_bias(n):
        dt = jnp.exp(jax.random.uniform(next(ks), (L, n), jnp.float32, math.log(1e-3), math.log(1e-1)))
        return dt + jnp.log(-jnp.expm1(-dt))

    def a_log(n):
        return jnp.log(jax.random.uniform(next(ks), (L, n), jnp.float32, 1.0, 16.0))

    return {
        "x_prompt": nrm((BATCH, SEQ, D_MODEL), 1.0),
        "x_sample": nrm((DEC_BATCH, DEC_SEQ, D_MODEL), 1.0),
        "state_ssd_conv": nrm((L, DEC_BATCH, CONV_K - 1, SSD_CONV_CH), 1.0),
        "state_ssd": nrm((L, DEC_BATCH, SSD_HEADS, SSD_HEAD_DIM, SSD_STATE), 0.1),
        "state_gdn_conv": nrm((L, DEC_BATCH, CONV_K - 1, GDN_CONV_CH), 1.0),
        "state_gdn": nrm((L, DEC_BATCH, GDN_HEADS, GDN_HEAD_DIM, GDN_HEAD_DIM), 0.1),
        "meta_tokens": nrm((N_META, D_MODEL), 1.0),
        "norm_mix_w": gain((L, D_MODEL)),
        "w_in": nrm((L, D_MODEL, IN_COLS), D_MODEL ** -0.5),
        "ssd_conv_w": nrm((L, CONV_K, SSD_CONV_CH), CONV_K ** -0.5),
        "ssd_conv_b": nrm((L, SSD_CONV_CH), 0.01),
        "ssd_dt_bias": dt_bias(SSD_HEADS),
        "ssd_a_log": a_log(SSD_HEADS),
        "ssd_d": gain((L, SSD_HEADS)),
        "ssd_norm_w": gain((L, SSD_D_INNER)),
        "gdn_conv_w": nrm((L, CONV_K, GDN_CONV_CH), CONV_K ** -0.5),
        "gdn_dt_bias": dt_bias(GDN_HEADS),
        "gdn_a_log": a_log(GDN_HEADS),
        "gdn_norm_w": gain((L, GDN_HEAD_DIM)),
        "w_out": nrm((L, MIX_WIDTH, D_MODEL), MIX_WIDTH ** -0.5),
        "norm_mlp_w": gain((L, D_MODEL)),
        "w_up": nrm((L, D_MODEL, D_FF), D_MODEL ** -0.5),
        "w_down": nrm((L, D_FF, D_MODEL), D_FF ** -0.5),
        "norm_f_w": gain((D_MODEL,)),
    }


def reference(x_prompt, x_sample, state_ssd_conv, state_ssd, state_gdn_conv, state_gdn,
              meta_tokens, norm_mix_w, w_in, ssd_conv_w, ssd_conv_b, ssd_dt_bias, ssd_a_log, ssd_d,
              ssd_norm_w, gdn_conv_w, gdn_dt_bias, gdn_a_log, gdn_norm_w, w_out, norm_mlp_w,
              w_up, w_down, norm_f_w):
    bp = x_prompt.shape[0]
    dtp = x_prompt.dtype
    hp = jnp.concatenate([jnp.broadcast_to(meta_tokens.astype(dtp)[None], (bp, N_META, D_MODEL)), x_prompt], axis=1)
    hs = x_sample
    zc_a = jnp.zeros((bp, CONV_K - 1, SSD_CONV_CH), dtp)
    zs_a = jnp.zeros((bp, SSD_HEADS, SSD_HEAD_DIM, SSD_STATE), dtp)
    zc_b = jnp.zeros((bp, CONV_K - 1, GDN_CONV_CH), dtp)
    zs_b = jnp.zeros((bp, GDN_HEADS, GDN_HEAD_DIM, GDN_HEAD_DIM), dtp)
    outs_p = ([], [], [], [])
    outs_s = ([], [], [], [])
    for i in range(DEPTH):
        lw = (norm_mix_w[i], w_in[i], ssd_conv_w[i], ssd_conv_b[i], ssd_dt_bias[i], ssd_a_log[i], ssd_d[i],
              ssd_norm_w[i], gdn_conv_w[i], gdn_dt_bias[i], gdn_a_log[i], gdn_norm_w[i], w_out[i],
              norm_mlp_w[i], w_up[i], w_down[i])
        hp, *st_p = _layer(hp, META_PAD, zc_a, zs_a, zc_b, zs_b, *lw)
        hs, *st_s = _layer(hs, 0, state_ssd_conv[i], state_ssd[i], state_gdn_conv[i], state_gdn[i], *lw)
        for lst, t in zip(outs_p, st_p):
            lst.append(t)
        for lst, t in zip(outs_s, st_s):
            lst.append(t)
    y_prompt = _rmsnorm(hp, norm_f_w)[:, N_META:]
    y_sample = _rmsnorm(hs, norm_f_w)
    ssd_conv_p, ssd_p, gdn_conv_p, gdn_p = (jnp.stack(t) for t in outs_p)
    ssd_conv_s, ssd_s, gdn_conv_s, gdn_s = (jnp.stack(t) for t in outs_s)
    return (y_prompt, y_sample, ssd_conv_p, ssd_p, gdn_conv_p, gdn_p, ssd_conv_s, ssd_s, gdn_conv_s, gdn_s)
```

```python
import functools

import jax
import jax.numpy as jnp
from jax import lax
from jax.experimental import pallas as pl
from jax.experimental.pallas import tpu as pltpu

F32 = jnp.float32
BF16 = jnp.bfloat16
HIGHEST = lax.Precision.HIGHEST

D_MODEL = 2048
CHUNK = 64
N_META = 16
CONV_K = 4
EPS = 1e-6

SSD_HEADS = 32
SSD_HEAD_DIM = 64
SSD_GROUPS = 4
SSD_STATE = 128
SSD_D_INNER = SSD_HEADS * SSD_HEAD_DIM
SSD_BC = 2 * SSD_GROUPS * SSD_STATE
SSD_CONV_CH = SSD_D_INNER + SSD_BC
SSD_PAIRS = SSD_HEADS // 2

GDN_HEADS = 16
GDN_HEAD_DIM = 128
GDN_WIDTH = GDN_HEADS * GDN_HEAD_DIM
GDN_CONV_CH = 3 * GDN_WIDTH
D_FF = 4 * D_MODEL

LANES = 128
WIDE = 2048
BLK_ZA, BLK_XS, BLK_Q, BLK_K, BLK_V, BLK_ZB, BLK_GA, BLK_GB = range(8)
P_COLS = 8 * WIDE + SSD_BC
BLK_BC = (8 * WIDE) // SSD_BC
SMALL_COLS = 2 * LANES

VMEM_LIMIT = 56 * 1024 * 1024


def _silu(x):
    return x * (1.0 / (1.0 + jnp.exp(-x)))


def _sigmoid(x):
    return 1.0 / (1.0 + jnp.exp(-x))


def _softplus(x):
    return jnp.maximum(x, 0.0) + jnp.log1p(jnp.exp(-jnp.abs(x)))


def _inproj_kernel(x_ref, nw_ref, w_ref, ws_ref, p_ref, ps_ref, u_s):
    j = pl.program_id(1)

    @pl.when(j == 0)
    def _():
        x = x_ref[...]
        ms = jnp.mean(x * x, axis=-1, keepdims=True)
        u = (x * lax.rsqrt(ms + EPS)) * nw_ref[...]
        u_s[...] = u.astype(BF16)
        ps_ref[...] = jnp.dot(u_s[...], ws_ref[...], preferred_element_type=F32)

    p_ref[...] = jnp.dot(u_s[...], w_ref[...], preferred_element_type=F32)


def _inproj(x2d, norm_w, w_main, w_small, tm, tn=1024):
    m = x2d.shape[0]
    grid = (m // tm, P_COLS // tn)
    return pl.pallas_call(
        _inproj_kernel,
        grid=grid,
        in_specs=[
            pl.BlockSpec((tm, D_MODEL), lambda i, j: (i, 0)),
            pl.BlockSpec((1, D_MODEL), lambda i, j: (0, 0)),
            pl.BlockSpec((D_MODEL, tn), lambda i, j: (0, j)),
            pl.BlockSpec((D_MODEL, SMALL_COLS), lambda i, j: (0, 0)),
        ],
        out_specs=[
            pl.BlockSpec((tm, tn), lambda i, j: (i, j)),
            pl.BlockSpec((tm, SMALL_COLS), lambda i, j: (i, 0)),
        ],
        out_shape=[
            jax.ShapeDtypeStruct((m, P_COLS), F32),
            jax.ShapeDtypeStruct((m, SMALL_COLS), F32),
        ],
        scratch_shapes=[pltpu.VMEM((tm, D_MODEL), BF16)],
        compiler_params=pltpu.CompilerParams(
            dimension_semantics=("arbitrary", "arbitrary"), vmem_limit_bytes=VMEM_LIMIT),
        name="inproj",
    )(x2d, norm_w, w_main, w_small)


def _mixer_kernel(za_ref, xs_ref, q_ref, k_ref, v_ref, zb_ref, bc_ref, sm_ref,
                  cva_ref, ssd_ref, cvb_ref, gdn_ref,
                  cwa_ref, cba_ref, cwb_ref, sbias_ref, alog_ref, dexp_ref, nwa_ref, nwb_ref,
                  eall_ref,
                  ya_ref, ob_ref, cva_o, ssd_o, cvb_o, gdn_o,
                  xpa, xpb, xs_s, bc_s, q_s, k_s, v_s, rb_s, cumt_s, colp_s, cb2_s, bmt_s,
                  ht_s, s_s, y_s, *, n_pad):
    c = pl.program_id(1)
    nc = pl.num_programs(1)
    T = CHUNK
    HIST = CONV_K - 1

    @pl.when(c == 0)
    def _():
        xpa[8 - HIST:8, :] = cva_ref[...]
        xpb[8 - HIST:8, :] = cvb_ref[...]

    @pl.when(c > 0)
    def _():
        xpa[8 - HIST:8, :] = xpa[8 + T - HIST:8 + T, :]
        xpb[8 - HIST:8, :] = xpb[8 + T - HIST:8 + T, :]

    xpa[8:8 + T, 0:SSD_D_INNER] = xs_ref[...]
    xpa[8:8 + T, SSD_D_INNER:SSD_CONV_CH] = bc_ref[...]
    xpb[8:8 + T, 0:GDN_WIDTH] = q_ref[...]
    xpb[8:8 + T, GDN_WIDTH:2 * GDN_WIDTH] = k_ref[...]
    xpb[8:8 + T, 2 * GDN_WIDTH:3 * GDN_WIDTH] = v_ref[...]

    def conv_silu(xp, w_ref, b_ref, lo, width, out_ref, out_lo):
        slab = 512
        for s in range(width // slab):
            a = lo + s * slab
            acc = xp[8:8 + T, a:a + slab] * w_ref[3:4, a:a + slab]
            for j in range(1, CONV_K):
                acc = acc + xp[8 - j:8 - j + T, a:a + slab] * w_ref[3 - j:4 - j, a:a + slab]
            if b_ref is not None:
                acc = acc + b_ref[:, a:a + slab]
            out_ref[:, out_lo + s * slab:out_lo + (s + 1) * slab] = _silu(acc)

    conv_silu(xpa, cwa_ref, cba_ref, 0, SSD_D_INNER, xs_s, 0)
    conv_silu(xpa, cwa_ref, cba_ref, SSD_D_INNER, SSD_BC, bc_s, 0)
    conv_silu(xpb, cwb_ref, None, 0, GDN_WIDTH, q_s, 0)
    conv_silu(xpb, cwb_ref, None, GDN_WIDTH, GDN_WIDTH, k_s, 0)
    conv_silu(xpb, cwb_ref, None, 2 * GDN_WIDTH, GDN_WIDTH, v_s, 0)

    @pl.when(c == nc - 1)
    def _():
        cva_o[...] = xpa[8 + T - HIST:8 + T, :]
        cvb_o[...] = xpb[8 + T - HIST:8 + T, :]

    @pl.when(c == 0)
    def _():
        for j in range(SSD_PAIRS):
            ht_s[:, j * LANES:(j + 1) * LANES] = ssd_ref[j].T
        s_s[...] = gdn_ref[...]

    lane = lax.broadcasted_iota(jnp.int32, (T, LANES), 1)
    row = lax.broadcasted_iota(jnp.int32, (T, LANES), 0)
    sp = _softplus(sm_ref[:, 0:LANES] + sbias_ref[...])
    sg = _sigmoid(sm_ref[:, LANES:2 * LANES])
    neg_a = -jnp.exp(alog_ref[...])
    used = lane < SSD_HEADS + GDN_HEADS
    gmat = jnp.where(used, sp * neg_a, 0.0)
    vmat = jnp.where(lane < SSD_HEADS, sp, jnp.where(used, sg, 0.0))
    if n_pad:
        live = row >= n_pad
        gmat = jnp.where(live, gmat, 0.0)
        vmat = jnp.where(live, vmat, 0.0)
    tr = lax.broadcasted_iota(jnp.int32, (T, T), 0)
    tc = lax.broadcasted_iota(jnp.int32, (T, T), 1)
    incl = tr >= tc
    strict = tr > tc
    tril = jnp.where(incl, 1.0, 0.0).astype(F32)
    cum = jnp.dot(tril, gmat, precision=HIGHEST, preferred_element_type=F32)
    cumt = jnp.concatenate([cum, jnp.zeros_like(cum)], axis=0).T
    cumt_s[...] = cumt
    rb_s[...] = jnp.dot(jnp.concatenate([cum, vmat], axis=0), eall_ref[...],
                        precision=HIGHEST, preferred_element_type=F32)
    for j in range(SSD_PAIRS):
        colp_s[j:j + 1, :] = jnp.concatenate([cumt[2 * j:2 * j + 1, 0:T], cumt[2 * j + 1:2 * j + 2, 0:T]], axis=1)

    for g in range(SSD_GROUPS):
        bm = bc_s[:, g * SSD_STATE:(g + 1) * SSD_STATE]
        cm = bc_s[:, (SSD_GROUPS + g) * SSD_STATE:(SSD_GROUPS + g + 1) * SSD_STATE]
        cb = lax.dot_general(cm, bm, (((1,), (1,)), ((), ())), preferred_element_type=F32)
        cb2_s[g] = jnp.concatenate([cb, cb], axis=1)
        bmt_s[g] = jnp.concatenate([bm, jnp.zeros_like(bm)], axis=0).T

    causal2 = row >= (lane & (T - 1))
    lo_half = lane < T

    def ssd_pair(j, carry):
        o = pl.multiple_of(j * LANES, LANES)
        g = j // (SSD_PAIRS // SSD_GROUPS)
        go = pl.multiple_of((SSD_GROUPS + g) * SSD_STATE, SSD_STATE)
        rowp = rb_s[0:T, pl.ds(o, LANES)]
        dtb = rb_s[T:2 * T, pl.ds(o, LANES)]
        colp = colp_s[pl.ds(j, 1), :]
        lmat = jnp.exp(jnp.where(causal2, rowp - colp, -jnp.inf))
        mp = cb2_s[g] * lmat
        xsp = xs_s[:, pl.ds(o, LANES)]
        xdt = xsp * dtb
        bd = jnp.concatenate([jnp.where(lo_half, xdt, 0.0), jnp.where(lo_half, 0.0, xdt)], axis=0)
        y = jnp.dot(mp, bd, preferred_element_type=F32)
        h_old = ht_s[:, pl.ds(o, LANES)]
        cm = bc_s[:, pl.ds(go, SSD_STATE)]
        y = y + jnp.dot(cm, h_old, preferred_element_type=F32) * jnp.exp(rowp)
        last = rowp[T - 1:T, :]
        xdec = xdt * jnp.exp(last - rowp)
        bmt = bmt_s[g][:, 0:T]
        ht_s[:, pl.ds(o, LANES)] = h_old * jnp.exp(last) + jnp.dot(bmt, xdec, preferred_element_type=F32)
        y = y + xsp * dexp_ref[:, pl.ds(o, LANES)]
        y_s[:, pl.ds(o, LANES)] = y * _silu(za_ref[:, pl.ds(o, LANES)])
        return carry

    lax.fori_loop(0, SSD_PAIRS, ssd_pair, 0)

    gw = SSD_D_INNER // SSD_GROUPS
    for g in range(SSD_GROUPS):
        yg = y_s[:, g * gw:(g + 1) * gw]
        ms = jnp.mean(yg * yg, axis=-1, keepdims=True)
        ya_ref[:, g * gw:(g + 1) * gw] = ((yg * lax.rsqrt(ms + EPS)) * nwa_ref[:, g * gw:(g + 1) * gw]).astype(BF16)

    eye = jnp.where(tr == tc, 1.0, 0.0).astype(F32)
    q_scale = GDN_HEAD_DIM ** -0.5

    def gdn_head(h, carry):
        o = pl.multiple_of(h * LANES, LANES)
        og = pl.multiple_of(SSD_D_INNER + h * LANES, LANES)
        qh = q_s[:, pl.ds(o, LANES)]
        kh = k_s[:, pl.ds(o, LANES)]
        vh = v_s[:, pl.ds(o, LANES)]
        qn = qh * (lax.rsqrt(jnp.sum(qh * qh, axis=-1, keepdims=True) + EPS) * q_scale)
        kn = kh * lax.rsqrt(jnp.sum(kh * kh, axis=-1, keepdims=True) + EPS)
        gcb = rb_s[0:T, pl.ds(og, LANES)]
        betab = rb_s[T:2 * T, pl.ds(og, LANES)]
        gct = cumt_s[pl.ds(SSD_HEADS + h, 1), :][:, 0:T]
        decay = jnp.exp(jnp.where(incl, gcb[:, 0:T] - gct, -jnp.inf))
        qk2 = lax.dot_general(jnp.concatenate([qn, kn], axis=0), kn, (((1,), (1,)), ((), ())),
                              preferred_element_type=F32)
        qk = qk2[0:T]
        kk = qk2[T:2 * T]
        amat = jnp.where(strict, betab[:, 0:T] * kk * decay, 0.0)
        tinv = eye - amat
        apow = amat
        for _ in range(5):
            apow = jnp.dot(apow, apow, precision=HIGHEST, preferred_element_type=F32)
            tinv = tinv + jnp.dot(tinv, apow, precision=HIGHEST, preferred_element_type=F32)
        eg = jnp.exp(gcb)
        rhs = jnp.concatenate([vh * betab, kn * (betab * eg)], axis=1)
        uw = jnp.dot(tinv, rhs, preferred_element_type=F32)
        u = uw[:, 0:LANES]
        w = uw[:, LANES:2 * LANES]
        s_old = s_s[h]
        wq = jnp.dot(jnp.concatenate([w, qn * eg], axis=0), s_old, preferred_element_type=F32)
        wv = u - wq[0:T]
        oh = wq[T:2 * T] + jnp.dot(qk * decay, wv, preferred_element_type=F32)
        last = gcb[T - 1:T, :]
        kdec = kn * jnp.exp(last - gcb)
        kdect = jnp.concatenate([kdec, jnp.zeros_like(kdec)], axis=0).T[:, 0:T]
        s_s[h] = s_old * jnp.exp(last) + jnp.dot(kdect, wv, preferred_element_type=F32)
        ms = jnp.mean(oh * oh, axis=-1, keepdims=True)
        on = (oh * lax.rsqrt(ms + EPS)) * nwb_ref[...]
        ob_ref[:, pl.ds(o, LANES)] = (on * _silu(zb_ref[:, pl.ds(o, LANES)])).astype(BF16)
        return carry

    lax.fori_loop(0, GDN_HEADS, gdn_head, 0)

    @pl.when(c == nc - 1)
    def _():
        for j in range(SSD_PAIRS):
            ssd_o[j] = ht_s[:, j * LANES:(j + 1) * LANES].T
        gdn_o[...] = s_s[...]


def _mixer(p, ps, conv_a, ssm_a, conv_b, ssm_b, prm, n_seq, seq_len, n_pad):
    nc = seq_len // CHUNK
    m = n_seq * seq_len
    T = CHUNK

    def rows(blk):
        return lambda b, c: (b * nc + c, blk)

    def state_map(arr):
        shared = arr.shape[0] == 1
        nd = arr.ndim
        return lambda b, c: ((0 if shared else b),) + (0,) * (nd - 1)

    def full(arr):
        nd = arr.ndim
        return pl.BlockSpec(arr.shape, lambda b, c: (0,) * nd)

    ssm_a4 = ssm_a.reshape(ssm_a.shape[0], SSD_PAIRS, 2 * SSD_HEAD_DIM, SSD_STATE)
    in_specs = [
        pl.BlockSpec((T, WIDE), rows(BLK_ZA)),
        pl.BlockSpec((T, WIDE), rows(BLK_XS)),
        pl.BlockSpec((T, WIDE), rows(BLK_Q)),
        pl.BlockSpec((T, WIDE), rows(BLK_K)),
        pl.BlockSpec((T, WIDE), rows(BLK_V)),
        pl.BlockSpec((T, WIDE), rows(BLK_ZB)),
        pl.BlockSpec((T, SSD_BC), rows(BLK_BC)),
        pl.BlockSpec((T, SMALL_COLS), rows(0)),
        pl.BlockSpec((None, CONV_K - 1, SSD_CONV_CH), state_map(conv_a)),
        pl.BlockSpec((None, SSD_PAIRS, 2 * SSD_HEAD_DIM, SSD_STATE), state_map(ssm_a4)),
        pl.BlockSpec((None, CONV_K - 1, GDN_CONV_CH), state_map(conv_b)),
        pl.BlockSpec((None, GDN_HEADS, GDN_HEAD_DIM, GDN_HEAD_DIM), state_map(ssm_b)),
    ] + [full(a) for a in prm]
    out_specs = [
        pl.BlockSpec((T, SSD_D_INNER), lambda b, c: (b * nc + c, 0)),
        pl.BlockSpec((T, GDN_WIDTH), lambda b, c: (b * nc + c, 0)),
        pl.BlockSpec((None, CONV_K - 1, SSD_CONV_CH), lambda b, c: (b, 0, 0)),
        pl.BlockSpec((None, SSD_PAIRS, 2 * SSD_HEAD_DIM, SSD_STATE), lambda b, c: (b, 0, 0, 0)),
        pl.BlockSpec((None, CONV_K - 1, GDN_CONV_CH), lambda b, c: (b, 0, 0)),
        pl.BlockSpec((None, GDN_HEADS, GDN_HEAD_DIM, GDN_HEAD_DIM), lambda b, c: (b, 0, 0, 0)),
    ]
    out_shape = [
        jax.ShapeDtypeStruct((m, SSD_D_INNER), BF16),
        jax.ShapeDtypeStruct((m, GDN_WIDTH), BF16),
        jax.ShapeDtypeStruct((n_seq, CONV_K - 1, SSD_CONV_CH), F32),
        jax.ShapeDtypeStruct((n_seq, SSD_PAIRS, 2 * SSD_HEAD_DIM, SSD_STATE), F32),
        jax.ShapeDtypeStruct((n_seq, CONV_K - 1, GDN_CONV_CH), F32),
        jax.ShapeDtypeStruct((n_seq, GDN_HEADS, GDN_HEAD_DIM, GDN_HEAD_DIM), F32),
    ]
    scratch = [
        pltpu.VMEM((8 + T, SSD_CONV_CH), F32),
        pltpu.VMEM((8 + T, GDN_CONV_CH), F32),
        pltpu.VMEM((T, SSD_D_INNER), F32),
        pltpu.VMEM((T, SSD_BC), F32),
        pltpu.VMEM((T, GDN_WIDTH), F32),
        pltpu.VMEM((T, GDN_WIDTH), F32),
        pltpu.VMEM((T, GDN_WIDTH), F32),
        pltpu.VMEM((2 * T, SSD_D_INNER + GDN_WIDTH), F32),
        pltpu.VMEM((LANES, LANES), F32),
        pltpu.VMEM((SSD_PAIRS, LANES), F32),
        pltpu.VMEM((SSD_GROUPS, T, LANES), F32),
        pltpu.VMEM((SSD_GROUPS, LANES, LANES), F32),
        pltpu.VMEM((SSD_STATE, SSD_D_INNER), F32),
        pltpu.VMEM((GDN_HEADS, GDN_HEAD_DIM, GDN_HEAD_DIM), F32),
        pltpu.VMEM((T, SSD_D_INNER), F32),
    ]
    outs = pl.pallas_call(
        functools.partial(_mixer_kernel, n_pad=n_pad),
        grid=(n_seq, nc),
        in_specs=in_specs,
        out_specs=out_specs,
        out_shape=out_shape,
        scratch_shapes=scratch,
        compiler_params=pltpu.CompilerParams(
            dimension_semantics=("arbitrary", "arbitrary"), vmem_limit_bytes=VMEM_LIMIT),
        name="mixer",
    )(p, p, p, p, p, p, p, ps, conv_a, ssm_a4, conv_b, ssm_b, *prm)
    ya, ob, cva, ssd, cvb, gdn = outs
    return ya, ob, cva, ssd.reshape(n_seq, SSD_HEADS, SSD_HEAD_DIM, SSD_STATE), cvb, gdn


def _outproj_kernel(x_ref, ya_ref, ob_ref, ga_ref, gb_ref, wa_ref, wb_ref, nw_ref, h_ref, n_ref):
    ma = jnp.dot(ya_ref[...], wa_ref[...], preferred_element_type=F32)
    mb = jnp.dot(ob_ref[...], wb_ref[...], preferred_element_type=F32)
    h = x_ref[...] + _sigmoid(ga_ref[...]) * ma + _sigmoid(gb_ref[...]) * mb
    h_ref[...] = h
    ms = jnp.mean(h * h, axis=-1, keepdims=True)
    n_ref[...] = ((h * lax.rsqrt(ms + EPS)) * nw_ref[...]).astype(BF16)


def _outproj(x2d, ya, ob, p, w_out_a, w_out_b, norm_w, tm):
    m = x2d.shape[0]
    const = lambda i: (0, 0)
    return pl.pallas_call(
        _outproj_kernel,
        grid=(m // tm,),
        in_specs=[
            pl.BlockSpec((tm, D_MODEL), lambda i: (i, 0)),
            pl.BlockSpec((tm, SSD_D_INNER), lambda i: (i, 0)),
            pl.BlockSpec((tm, GDN_WIDTH), lambda i: (i, 0)),
            pl.BlockSpec((tm, WIDE), lambda i: (i, BLK_GA)),
            pl.BlockSpec((tm, WIDE), lambda i: (i, BLK_GB)),
            pl.BlockSpec((SSD_D_INNER, D_MODEL), const, pipeline_mode=pl.Buffered(1)),
            pl.BlockSpec((GDN_WIDTH, D_MODEL), const, pipeline_mode=pl.Buffered(1)),
            pl.BlockSpec((1, D_MODEL), const),
        ],
        out_specs=[
            pl.BlockSpec((tm, D_MODEL), lambda i: (i, 0)),
            pl.BlockSpec((tm, D_MODEL), lambda i: (i, 0)),
        ],
        out_shape=[
            jax.ShapeDtypeStruct((m, D_MODEL), F32),
            jax.ShapeDtypeStruct((m, D_MODEL), BF16),
        ],
        compiler_params=pltpu.CompilerParams(
            dimension_semantics=("arbitrary",), vmem_limit_bytes=VMEM_LIMIT),
        name="outproj",
    )(x2d, ya, ob, p, p, w_out_a, w_out_b, norm_w)


def _mlp_kernel(n_ref, h_ref, wu_ref, wd_ref, nf_ref, y_ref, acc_s):
    j = pl.program_id(1)
    hid = jnp.dot(n_ref[...], wu_ref[...], preferred_element_type=F32)
    hid = jnp.square(jnp.maximum(hid, 0.0)).astype(BF16)
    part = jnp.dot(hid, wd_ref[...], preferred_element_type=F32)

    @pl.when(j == 0)
    def _():
        acc_s[...] = h_ref[...] + part

    @pl.when(j > 0)
    def _():
        acc_s[...] += part

    @pl.when(j == pl.num_programs(1) - 1)
    def _():
        h = acc_s[...]
        ms = jnp.mean(h * h, axis=-1, keepdims=True)
        y_ref[...] = (h * lax.rsqrt(ms + EPS)) * nf_ref[...]


def _mlp(n, h, w_up, w_down, norm_f, tm, tf=1024):
    m = n.shape[0]
    return pl.pallas_call(
        _mlp_kernel,
        grid=(m // tm, D_FF // tf),
        in_specs=[
            pl.BlockSpec((tm, D_MODEL), lambda i, j: (i, 0)),
            pl.BlockSpec((tm, D_MODEL), lambda i, j: (i, 0)),
            pl.BlockSpec((D_MODEL, tf), lambda i, j: (0, j)),
            pl.BlockSpec((tf, D_MODEL), lambda i, j: (j, 0)),
            pl.BlockSpec((1, D_MODEL), lambda i, j: (0, 0)),
        ],
        out_specs=pl.BlockSpec((tm, D_MODEL), lambda i, j: (i, 0)),
        out_shape=jax.ShapeDtypeStruct((m, D_MODEL), F32),
        scratch_shapes=[pltpu.VMEM((tm, D_MODEL), F32)],
        compiler_params=pltpu.CompilerParams(
            dimension_semantics=("arbitrary", "arbitrary"), vmem_limit_bytes=VMEM_LIMIT),
        name="mlp",
    )(n, h, w_up, w_down, norm_f)


def _expansion_matrix():
    r = jnp.arange(LANES)[:, None]
    col = jnp.arange(SSD_D_INNER + GDN_WIDTH)[None, :]
    ssd = (col < SSD_D_INNER) & (col // SSD_HEAD_DIM == r)
    gdn = (col >= SSD_D_INNER) & ((col - SSD_D_INNER) // GDN_HEAD_DIM + SSD_HEADS == r)
    return (ssd | gdn).astype(F32)


def _row_tile(m, cap):
    t = min(m, cap)
    while m % t:
        t //= 2
    return t


def kernel(x_prompt, x_sample, state_ssd_conv, state_ssd, state_gdn_conv, state_gdn, meta_tokens, norm_mix_w, w_in, ssd_conv_w, ssd_conv_b, ssd_dt_bias, ssd_a_log, ssd_d, ssd_norm_w, gdn_conv_w, gdn_dt_bias, gdn_a_log, gdn_norm_w, w_out, norm_mlp_w, w_up, w_down, norm_f_w):
    bp, sp_len, _ = x_prompt.shape
    bs, ss_len, _ = x_sample.shape
    assert w_in.shape[0] == 1, "single layer"
    assert sp_len % CHUNK == 0 and ss_len % CHUNK == 0 and N_META <= CHUNK

    wi = w_in[0]
    z_a, xbc_a, dt_a, qkv_b, z_b, beta_b, a_b, gate_a, gate_b = jnp.split(
        wi, [2048, 5120, 5152, 11296, 13344, 13360, 13376, 15424], axis=1)
    xs_w, bc_w = xbc_a[:, :SSD_D_INNER], xbc_a[:, SSD_D_INNER:]
    q_w, k_w, v_w = jnp.split(qkv_b, 3, axis=1)
    w_main = jnp.concatenate([z_a, xs_w, q_w, k_w, v_w, z_b, gate_a, gate_b, bc_w], axis=1).astype(BF16)
    zc = lambda n: jnp.zeros((D_MODEL, n), wi.dtype)
    w_small = jnp.concatenate([dt_a, a_b, zc(LANES - 48), zc(32), beta_b, zc(LANES - 48)], axis=1).astype(BF16)
    w_out_a = w_out[0, :SSD_D_INNER].astype(BF16)
    w_out_b = w_out[0, SSD_D_INNER:].astype(BF16)
    w_up_b = w_up[0].astype(BF16)
    w_down_b = w_down[0].astype(BF16)

    zrow = lambda n: jnp.zeros((1, n), F32)
    sbias = jnp.concatenate([ssd_dt_bias[0][None], gdn_dt_bias[0][None], zrow(LANES - 48)], axis=1)
    alog = jnp.concatenate([ssd_a_log[0][None], gdn_a_log[0][None], zrow(LANES - 48)], axis=1)
    dexp = jnp.repeat(ssd_d[0], SSD_HEAD_DIM)[None]
    prm = (ssd_conv_w[0], ssd_conv_b[0][None], gdn_conv_w[0], sbias, alog, dexp,
           ssd_norm_w[0][None], gdn_norm_w[0][None], _expansion_matrix())
    nmix = norm_mix_w[0][None]
    nmlp = norm_mlp_w[0][None]
    nf = norm_f_w[None]

    def front(x2d, n_seq, seq_len, n_pad, conv_a, ssm_a, conv_b, ssm_b):
        p, ps = _inproj(x2d, nmix, w_main, w_small, _row_tile(x2d.shape[0], 1024))
        return p, _mixer(p, ps, conv_a, ssm_a, conv_b, ssm_b, prm, n_seq, seq_len, n_pad)

    def back(x2d, p, ya, ob):
        h, n = _outproj(x2d, ya, ob, p, w_out_a, w_out_b, nmlp, _row_tile(x2d.shape[0], 256))
        return _mlp(n, h, w_up_b, w_down_b, nf, _row_tile(x2d.shape[0], 512))

    dtp = x_prompt.dtype
    n_pad = CHUNK - N_META
    x_meta = jnp.concatenate([jnp.zeros((n_pad, D_MODEL), dtp), meta_tokens.astype(dtp)], axis=0)
    _, (_, _, cva_m, ssd_m, cvb_m, gdn_m) = front(
        x_meta, 1, CHUNK, n_pad,
        jnp.zeros((1, CONV_K - 1, SSD_CONV_CH), dtp), jnp.zeros((1, SSD_HEADS, SSD_HEAD_DIM, SSD_STATE), dtp),
        jnp.zeros((1, CONV_K - 1, GDN_CONV_CH), dtp), jnp.zeros((1, GDN_HEADS, GDN_HEAD_DIM, GDN_HEAD_DIM), dtp))

    xp2 = x_prompt.reshape(bp * sp_len, D_MODEL)
    p_p, (ya_p, ob_p, cva_p, ssd_p, cvb_p, gdn_p) = front(xp2, bp, sp_len, 0, cva_m, ssd_m, cvb_m, gdn_m)
    y_prompt = back(xp2, p_p, ya_p, ob_p).reshape(bp, sp_len, D_MODEL)

    xs2 = x_sample.reshape(bs * ss_len, D_MODEL)
    p_s, (ya_s, ob_s, cva_s, ssd_s, cvb_s, gdn_s) = front(
        xs2, bs, ss_len, 0, state_ssd_conv[0], state_ssd[0], state_gdn_conv[0], state_gdn[0])
    y_sample = back(xs2, p_s, ya_s, ob_s).reshape(bs, ss_len, D_MODEL)

    return (y_prompt, y_sample, cva_p[None], ssd_p[None], cvb_p[None], gdn_p[None],
            cva_s[None], ssd_s[None], cvb_s[None], gdn_s[None])
```

```python
import functools

import jax
import jax.numpy as jnp
from jax import lax
from jax.experimental import pallas as pl
from jax.experimental.pallas import tpu as pltpu

F32 = jnp.float32
BF16 = jnp.bfloat16
HIGHEST = lax.Precision.HIGHEST

D_MODEL = 2048
CHUNK = 64
N_META = 16
CONV_K = 4
EPS = 1e-6

SSD_HEADS = 32
SSD_HEAD_DIM = 64
SSD_GROUPS = 4
SSD_STATE = 128
SSD_D_INNER = SSD_HEADS * SSD_HEAD_DIM
SSD_BC = 2 * SSD_GROUPS * SSD_STATE
SSD_CONV_CH = SSD_D_INNER + SSD_BC
SSD_PAIRS = SSD_HEADS // 2

GDN_HEADS = 16
GDN_HEAD_DIM = 128
GDN_WIDTH = GDN_HEADS * GDN_HEAD_DIM
GDN_CONV_CH = 3 * GDN_WIDTH
D_FF = 4 * D_MODEL

LANES = 128
WIDE = 2048
BLK_ZA, BLK_XS, BLK_Q, BLK_K, BLK_V, BLK_ZB, BLK_GA, BLK_GB = range(8)
P_COLS = 8 * WIDE + SSD_BC
BLK_BC = (8 * WIDE) // SSD_BC
SMALL_COLS = 2 * LANES

VMEM_LIMIT = 56 * 1024 * 1024


def _silu(x):
    return x * (1.0 / (1.0 + jnp.exp(-x)))


def _sigmoid(x):
    return 1.0 / (1.0 + jnp.exp(-x))


def _softplus(x):
    return jnp.maximum(x, 0.0) + jnp.log1p(jnp.exp(-jnp.abs(x)))


def _bdot(a, b):
    return jnp.dot(a.astype(BF16), b.astype(BF16), preferred_element_type=F32)


def _bdot_nt(a, b):
    return lax.dot_general(a.astype(BF16), b.astype(BF16), (((1,), (1,)), ((), ())), preferred_element_type=F32)


def _bdot_tn(a, b):
    return lax.dot_general(a.astype(BF16), b.astype(BF16), (((0,), (0,)), ((), ())), preferred_element_type=F32)


def _inproj_kernel(x_ref, nw_ref, w_ref, ws_ref, p_ref, ps_ref, u_s):
    j = pl.program_id(1)

    @pl.when(j == 0)
    def _():
        x = x_ref[...]
        ms = jnp.mean(x * x, axis=-1, keepdims=True)
        u = (x * lax.rsqrt(ms + EPS)) * nw_ref[...]
        u_s[...] = u.astype(BF16)
        ps_ref[...] = jnp.dot(u_s[...], ws_ref[...], preferred_element_type=F32)

    p_ref[...] = jnp.dot(u_s[...], w_ref[...], preferred_element_type=F32)


def _inproj(x2d, norm_w, w_main, w_small, tm, tn=1024):
    m = x2d.shape[0]
    grid = (m // tm, P_COLS // tn)
    return pl.pallas_call(
        _inproj_kernel,
        grid=grid,
        in_specs=[
            pl.BlockSpec((tm, D_MODEL), lambda i, j: (i, 0)),
            pl.BlockSpec((1, D_MODEL), lambda i, j: (0, 0)),
            pl.BlockSpec((D_MODEL, tn), lambda i, j: (0, j)),
            pl.BlockSpec((D_MODEL, SMALL_COLS), lambda i, j: (0, 0)),
        ],
        out_specs=[
            pl.BlockSpec((tm, tn), lambda i, j: (i, j)),
            pl.BlockSpec((tm, SMALL_COLS), lambda i, j: (i, 0)),
        ],
        out_shape=[
            jax.ShapeDtypeStruct((m, P_COLS), F32),
            jax.ShapeDtypeStruct((m, SMALL_COLS), F32),
        ],
        scratch_shapes=[pltpu.VMEM((tm, D_MODEL), BF16)],
        compiler_params=pltpu.CompilerParams(
            dimension_semantics=("arbitrary", "arbitrary"), vmem_limit_bytes=VMEM_LIMIT),
        name="inproj",
    )(x2d, norm_w, w_main, w_small)


def _mixer_kernel(za_ref, xs_ref, q_ref, k_ref, v_ref, zb_ref, bc_ref, sm_ref,
                  cva_ref, ssd_ref, cvb_ref, gdn_ref,
                  cwa_ref, cba_ref, cwb_ref, sbias_ref, alog_ref, dexp_ref, nwa_ref, nwb_ref,
                  ya_ref, ob_ref, cva_o, ssd_o, cvb_o, gdn_o,
                  xpa, xpb, xs_s, bc_s, q_s, k_s, v_s, ht_s, s_s, y_s, *, n_pad):
    c = pl.program_id(1)
    nc = pl.num_programs(1)
    T = CHUNK
    HIST = CONV_K - 1

    @pl.when(c == 0)
    def _():
        xpa[8 - HIST:8, :] = cva_ref[...]
        xpb[8 - HIST:8, :] = cvb_ref[...]

    @pl.when(c > 0)
    def _():
        xpa[8 - HIST:8, :] = xpa[8 + T - HIST:8 + T, :]
        xpb[8 - HIST:8, :] = xpb[8 + T - HIST:8 + T, :]

    xpa[8:8 + T, 0:SSD_D_INNER] = xs_ref[...]
    xpa[8:8 + T, SSD_D_INNER:SSD_CONV_CH] = bc_ref[...]
    xpb[8:8 + T, 0:GDN_WIDTH] = q_ref[...]
    xpb[8:8 + T, GDN_WIDTH:2 * GDN_WIDTH] = k_ref[...]
    xpb[8:8 + T, 2 * GDN_WIDTH:3 * GDN_WIDTH] = v_ref[...]

    def conv_silu(xp, w_ref, b_ref, lo, width, out_ref, out_lo):
        slab = 512
        for s in range(width // slab):
            a = lo + s * slab
            acc = xp[8:8 + T, a:a + slab] * w_ref[3:4, a:a + slab]
            for j in range(1, CONV_K):
                acc = acc + xp[8 - j:8 - j + T, a:a + slab] * w_ref[3 - j:4 - j, a:a + slab]
            if b_ref is not None:
                acc = acc + b_ref[:, a:a + slab]
            out_ref[:, out_lo + s * slab:out_lo + (s + 1) * slab] = _silu(acc)

    conv_silu(xpa, cwa_ref, cba_ref, 0, SSD_D_INNER, xs_s, 0)
    conv_silu(xpa, cwa_ref, cba_ref, SSD_D_INNER, SSD_BC, bc_s, 0)
    conv_silu(xpb, cwb_ref, None, 0, GDN_WIDTH, q_s, 0)
    conv_silu(xpb, cwb_ref, None, GDN_WIDTH, GDN_WIDTH, k_s, 0)
    conv_silu(xpb, cwb_ref, None, 2 * GDN_WIDTH, GDN_WIDTH, v_s, 0)

    @pl.when(c == nc - 1)
    def _():
        cva_o[...] = xpa[8 + T - HIST:8 + T, :]
        cvb_o[...] = xpb[8 + T - HIST:8 + T, :]

    @pl.when(c == 0)
    def _():
        for j in range(SSD_PAIRS):
            ht_s[:, j * LANES:(j + 1) * LANES] = ssd_ref[j].T
        s_s[...] = gdn_ref[...]

    lane = lax.broadcasted_iota(jnp.int32, (T, LANES), 1)
    row = lax.broadcasted_iota(jnp.int32, (T, LANES), 0)
    sp = _softplus(sm_ref[:, 0:LANES] + sbias_ref[...])
    sg = _sigmoid(sm_ref[:, LANES:2 * LANES])
    neg_a = -jnp.exp(alog_ref[...])
    used = lane < SSD_HEADS + GDN_HEADS
    gmat = jnp.where(used, sp * neg_a, 0.0)
    vmat = jnp.where(lane < SSD_HEADS, sp, jnp.where(used, sg, 0.0))
    if n_pad:
        live = row >= n_pad
        gmat = jnp.where(live, gmat, 0.0)
        vmat = jnp.where(live, vmat, 0.0)
    tr = lax.broadcasted_iota(jnp.int32, (T, T), 0)
    tc = lax.broadcasted_iota(jnp.int32, (T, T), 1)
    tril = jnp.where(tr >= tc, 1.0, 0.0).astype(F32)
    cum = jnp.dot(tril, gmat, precision=HIGHEST, preferred_element_type=F32)
    zpad = jnp.zeros((T, LANES), F32)
    cumt = jnp.concatenate([cum, zpad], axis=0).T
    vmt = jnp.concatenate([vmat, zpad], axis=0).T
    last_all = cum[T - 1:T, :]
    wsm = jnp.exp(last_all - cum) * vmat

    def lane_bcast(m, col):
        return jnp.broadcast_to(m[:, col:col + 1], (T, LANES))

    def pair_cols(mt, r0, r1):
        return jnp.concatenate([mt[r0:r0 + 1, 0:T], mt[r1:r1 + 1, 0:T]], axis=1)

    lo_half = lane < T
    scol = lane & (T - 1)
    incl2 = row >= scol
    strict2 = row > scol
    mask_lo = jnp.where(lo_half, 1.0, 0.0).astype(BF16)
    mask_hi = jnp.where(lo_half, 0.0, 1.0).astype(BF16)

    def blockdiag(yp):
        yb = yp.astype(BF16)
        return jnp.concatenate([yb * mask_lo, yb * mask_hi], axis=0)

    gw = SSD_D_INNER // SSD_GROUPS
    ppg = SSD_PAIRS // SSD_GROUPS
    for g in range(SSD_GROUPS):
        bm = bc_s[:, g * SSD_STATE:(g + 1) * SSD_STATE]
        cm = bc_s[:, (SSD_GROUPS + g) * SSD_STATE:(SSD_GROUPS + g + 1) * SSD_STATE]
        bmb = bm.astype(BF16)
        cb2 = _bdot_nt(cm, jnp.concatenate([bmb, bmb], axis=0))
        h_old = ht_s[:, g * gw:(g + 1) * gw]
        yoff = _bdot(cm, h_old)
        xw_parts = []
        elast_parts = []
        for jj in range(ppg):
            j = g * ppg + jj
            r0, r1 = 2 * j, 2 * j + 1
            sl = slice(j * LANES, (j + 1) * LANES)
            rowp = jnp.where(lo_half, lane_bcast(cum, r0), lane_bcast(cum, r1))
            wb = jnp.where(lo_half, lane_bcast(wsm, r0), lane_bcast(wsm, r1))
            colp = pair_cols(cumt, r0, r1)
            dtrow = pair_cols(vmt, r0, r1)
            lmat = jnp.exp(jnp.where(incl2, rowp - colp, -jnp.inf))
            mp = cb2 * lmat * dtrow
            xsp = xs_s[:, sl]
            e1 = jnp.exp(rowp)
            y = _bdot(mp, blockdiag(xsp)) + yoff[:, jj * LANES:(jj + 1) * LANES] * e1 + xsp * dexp_ref[:, sl]
            y_s[:, sl] = y * _silu(za_ref[:, sl])
            xw_parts.append((xsp * wb).astype(BF16))
            elast_parts.append(e1[T - 1:T, :])
        xw = jnp.concatenate(xw_parts, axis=1)
        elast = jnp.concatenate(elast_parts, axis=1)
        ht_s[:, g * gw:(g + 1) * gw] = h_old * elast + _bdot_tn(bmb, xw)
        yg = y_s[:, g * gw:(g + 1) * gw]
        ms = jnp.mean(yg * yg, axis=-1, keepdims=True)
        ya_ref[:, g * gw:(g + 1) * gw] = ((yg * lax.rsqrt(ms + EPS)) * nwa_ref[:, g * gw:(g + 1) * gw]).astype(BF16)

    q_scale = GDN_HEAD_DIM ** -0.5
    eye2 = jnp.where(row == scol, 1.0, 0.0).astype(F32)
    blk16 = (row // 16) == (scol // 16)
    blk32 = (row // 32) == (scol // 32)
    zb = jnp.zeros((T, LANES), BF16)
    zb2 = jnp.zeros((T, 2 * LANES), BF16)

    npair = GDN_HEADS // 2
    qn, kn, vv, gcb, betab, eg = [], [], [], [], [], []
    for h in range(GDN_HEADS):
        sl = slice(h * LANES, (h + 1) * LANES)
        qh = q_s[:, sl]
        kh = k_s[:, sl]
        qn.append(qh * (lax.rsqrt(jnp.sum(qh * qh, axis=-1, keepdims=True) + EPS) * q_scale))
        kn.append(kh * lax.rsqrt(jnp.sum(kh * kh, axis=-1, keepdims=True) + EPS))
        vv.append(v_s[:, sl])
        gcb.append(lane_bcast(cum, SSD_HEADS + h))
        betab.append(lane_bcast(vmat, SSD_HEADS + h))
        eg.append(jnp.exp(gcb[-1]))

    decayp, qkp, ap = [], [], []
    for i in range(npair):
        h0, h1 = 2 * i, 2 * i + 1
        growp = jnp.where(lo_half, gcb[h0], gcb[h1])
        betap = jnp.where(lo_half, betab[h0], betab[h1])
        gcolp = pair_cols(cumt, SSD_HEADS + h0, SSD_HEADS + h1)
        decayp.append(jnp.exp(jnp.where(incl2, growp - gcolp, -jnp.inf)))
        knb = [kn[h0].astype(BF16), kn[h1].astype(BF16)]
        lhs = jnp.concatenate([jnp.concatenate([qn[h0].astype(BF16), qn[h1].astype(BF16)], axis=1),
                               jnp.concatenate(knb, axis=1)], axis=0)
        rhst = jnp.concatenate([jnp.concatenate([knb[0], zb], axis=1),
                                jnp.concatenate([zb, knb[1]], axis=1)], axis=0)
        qkkk = _bdot_nt(lhs, rhst)
        qkp.append(qkkk[0:T])
        ap.append(jnp.where(strict2, betap * qkkk[T:2 * T] * decayp[i], 0.0))

    d = [jnp.where(blk16, a, 0.0) for a in ap]
    pw = [_bdot(di, blockdiag(di)) for di in d]
    x = [eye2 - di for di in d]
    for lvl in range(3):
        bdp = [blockdiag(p_) for p_ in pw]
        if lvl < 2:
            r = [_bdot(jnp.concatenate([x[i], pw[i]], axis=0), bdp[i]) for i in range(npair)]
            x = [x[i] + r[i][0:T] for i in range(npair)]
            pw = [r[i][T:2 * T] for i in range(npair)]
        else:
            x = [x[i] + _bdot(x[i], bdp[i]) for i in range(npair)]
    for lvl in range(2):
        if lvl == 0:
            off = [jnp.where(blk32 & ~blk16, a, 0.0) for a in ap]
        else:
            off = [jnp.where(blk32, 0.0, a) for a in ap]
        y = [_bdot(x[i], blockdiag(off[i])) for i in range(npair)]
        x = [x[i] - _bdot(y[i], blockdiag(x[i])) for i in range(npair)]

    uw = []
    for i in range(npair):
        rhs = [jnp.concatenate([(vv[h] * betab[h]).astype(BF16), (kn[h] * (betab[h] * eg[h])).astype(BF16)], axis=1)
               for h in (2 * i, 2 * i + 1)]
        bdr = jnp.concatenate([jnp.concatenate([rhs[0], zb2], axis=1),
                               jnp.concatenate([zb2, rhs[1]], axis=1)], axis=0)
        uw.append(_bdot(x[i], bdr))

    wv, qs = [], []
    for h in range(GDN_HEADS):
        i, e = divmod(h, 2)
        u = uw[i][:, e * 2 * LANES:e * 2 * LANES + LANES]
        w = uw[i][:, e * 2 * LANES + LANES:(e + 1) * 2 * LANES]
        wq = _bdot(jnp.concatenate([w, qn[h] * eg[h]], axis=0), s_s[h])
        wv.append((u - wq[0:T]).astype(BF16))
        qs.append(wq[T:2 * T])

    oo = []
    for i in range(npair):
        bdw = jnp.concatenate([jnp.concatenate([wv[2 * i], zb], axis=1),
                               jnp.concatenate([zb, wv[2 * i + 1]], axis=1)], axis=0)
        oo.append(_bdot(qkp[i] * decayp[i], bdw))
    for h in range(GDN_HEADS):
        i, e = divmod(h, 2)
        sl = slice(h * LANES, (h + 1) * LANES)
        oh = qs[h] + oo[i][:, e * LANES:(e + 1) * LANES]
        last = gcb[h][T - 1:T, :]
        kdec = kn[h] * jnp.exp(last - gcb[h])
        s_s[h] = s_s[h] * eg[h][T - 1:T, :] + _bdot_tn(kdec, wv[h])
        ms = jnp.mean(oh * oh, axis=-1, keepdims=True)
        on = (oh * lax.rsqrt(ms + EPS)) * nwb_ref[...]
        ob_ref[:, sl] = (on * _silu(zb_ref[:, sl])).astype(BF16)

    @pl.when(c == nc - 1)
    def _():
        for j in range(SSD_PAIRS):
            ssd_o[j] = ht_s[:, j * LANES:(j + 1) * LANES].T
        gdn_o[...] = s_s[...]


def _mixer(p, ps, conv_a, ssm_a, conv_b, ssm_b, prm, n_seq, seq_len, n_pad):
    nc = seq_len // CHUNK
    m = n_seq * seq_len
    T = CHUNK

    def rows(blk):
        return lambda b, c: (b * nc + c, blk)

    def state_map(arr):
        shared = arr.shape[0] == 1
        nd = arr.ndim
        return lambda b, c: ((0 if shared else b),) + (0,) * (nd - 1)

    def full(arr):
        nd = arr.ndim
        return pl.BlockSpec(arr.shape, lambda b, c: (0,) * nd)

    ssm_a4 = ssm_a.reshape(ssm_a.shape[0], SSD_PAIRS, 2 * SSD_HEAD_DIM, SSD_STATE)
    in_specs = [
        pl.BlockSpec((T, WIDE), rows(BLK_ZA)),
        pl.BlockSpec((T, WIDE), rows(BLK_XS)),
        pl.BlockSpec((T, WIDE), rows(BLK_Q)),
        pl.BlockSpec((T, WIDE), rows(BLK_K)),
        pl.BlockSpec((T, WIDE), rows(BLK_V)),
        pl.BlockSpec((T, WIDE), rows(BLK_ZB)),
        pl.BlockSpec((T, SSD_BC), rows(BLK_BC)),
        pl.BlockSpec((T, SMALL_COLS), rows(0)),
        pl.BlockSpec((None, CONV_K - 1, SSD_CONV_CH), state_map(conv_a)),
        pl.BlockSpec((None, SSD_PAIRS, 2 * SSD_HEAD_DIM, SSD_STATE), state_map(ssm_a4)),
        pl.BlockSpec((None, CONV_K - 1, GDN_CONV_CH), state_map(conv_b)),
        pl.BlockSpec((None, GDN_HEADS, GDN_HEAD_DIM, GDN_HEAD_DIM), state_map(ssm_b)),
    ] + [full(a) for a in prm]
    out_specs = [
        pl.BlockSpec((T, SSD_D_INNER), lambda b, c: (b * nc + c, 0)),
        pl.BlockSpec((T, GDN_WIDTH), lambda b, c: (b * nc + c, 0)),
        pl.BlockSpec((None, CONV_K - 1, SSD_CONV_CH), lambda b, c: (b, 0, 0)),
        pl.BlockSpec((None, SSD_PAIRS, 2 * SSD_HEAD_DIM, SSD_STATE), lambda b, c: (b, 0, 0, 0)),
        pl.BlockSpec((None, CONV_K - 1, GDN_CONV_CH), lambda b, c: (b, 0, 0)),
        pl.BlockSpec((None, GDN_HEADS, GDN_HEAD_DIM, GDN_HEAD_DIM), lambda b, c: (b, 0, 0, 0)),
    ]
    out_shape = [
        jax.ShapeDtypeStruct((m, SSD_D_INNER), BF16),
        jax.ShapeDtypeStruct((m, GDN_WIDTH), BF16),
        jax.ShapeDtypeStruct((n_seq, CONV_K - 1, SSD_CONV_CH), F32),
        jax.ShapeDtypeStruct((n_seq, SSD_PAIRS, 2 * SSD_HEAD_DIM, SSD_STATE), F32),
        jax.ShapeDtypeStruct((n_seq, CONV_K - 1, GDN_CONV_CH), F32),
        jax.ShapeDtypeStruct((n_seq, GDN_HEADS, GDN_HEAD_DIM, GDN_HEAD_DIM), F32),
    ]
    scratch = [
        pltpu.VMEM((8 + T, SSD_CONV_CH), F32),
        pltpu.VMEM((8 + T, GDN_CONV_CH), F32),
        pltpu.VMEM((T, SSD_D_INNER), F32),
        pltpu.VMEM((T, SSD_BC), F32),
        pltpu.VMEM((T, GDN_WIDTH), F32),
        pltpu.VMEM((T, GDN_WIDTH), F32),
        pltpu.VMEM((T, GDN_WIDTH), F32),
        pltpu.VMEM((SSD_STATE, SSD_D_INNER), F32),
        pltpu.VMEM((GDN_HEADS, GDN_HEAD_DIM, GDN_HEAD_DIM), F32),
        pltpu.VMEM((T, SSD_D_INNER), F32),
    ]
    outs = pl.pallas_call(
        functools.partial(_mixer_kernel, n_pad=n_pad),
        grid=(n_seq, nc),
        in_specs=in_specs,
        out_specs=out_specs,
        out_shape=out_shape,
        scratch_shapes=scratch,
        compiler_params=pltpu.CompilerParams(
            dimension_semantics=("arbitrary", "arbitrary"), vmem_limit_bytes=VMEM_LIMIT),
        name="mixer",
    )(p, p, p, p, p, p, p, ps, conv_a, ssm_a4, conv_b, ssm_b, *prm)
    ya, ob, cva, ssd, cvb, gdn = outs
    return ya, ob, cva, ssd.reshape(n_seq, SSD_HEADS, SSD_HEAD_DIM, SSD_STATE), cvb, gdn


def _outproj_kernel(x_ref, ya_ref, ob_ref, ga_ref, gb_ref, wa_ref, wb_ref, nw_ref, h_ref, n_ref):
    ma = jnp.dot(ya_ref[...], wa_ref[...], preferred_element_type=F32)
    mb = jnp.dot(ob_ref[...], wb_ref[...], preferred_element_type=F32)
    h = x_ref[...] + _sigmoid(ga_ref[...]) * ma + _sigmoid(gb_ref[...]) * mb
    h_ref[...] = h
    ms = jnp.mean(h * h, axis=-1, keepdims=True)
    n_ref[...] = ((h * lax.rsqrt(ms + EPS)) * nw_ref[...]).astype(BF16)


def _outproj(x2d, ya, ob, p, w_out_a, w_out_b, norm_w, tm):
    m = x2d.shape[0]
    const = lambda i: (0, 0)
    return pl.pallas_call(
        _outproj_kernel,
        grid=(m // tm,),
        in_specs=[
            pl.BlockSpec((tm, D_MODEL), lambda i: (i, 0)),
            pl.BlockSpec((tm, SSD_D_INNER), lambda i: (i, 0)),
            pl.BlockSpec((tm, GDN_WIDTH), lambda i: (i, 0)),
            pl.BlockSpec((tm, WIDE), lambda i: (i, BLK_GA)),
            pl.BlockSpec((tm, WIDE), lambda i: (i, BLK_GB)),
            pl.BlockSpec((SSD_D_INNER, D_MODEL), const, pipeline_mode=pl.Buffered(1)),
            pl.BlockSpec((GDN_WIDTH, D_MODEL), const, pipeline_mode=pl.Buffered(1)),
            pl.BlockSpec((1, D_MODEL), const),
        ],
        out_specs=[
            pl.BlockSpec((tm, D_MODEL), lambda i: (i, 0)),
            pl.BlockSpec((tm, D_MODEL), lambda i: (i, 0)),
        ],
        out_shape=[
            jax.ShapeDtypeStruct((m, D_MODEL), F32),
            jax.ShapeDtypeStruct((m, D_MODEL), BF16),
        ],
        compiler_params=pltpu.CompilerParams(
            dimension_semantics=("arbitrary",), vmem_limit_bytes=VMEM_LIMIT),
        name="outproj",
    )(x2d, ya, ob, p, p, w_out_a, w_out_b, norm_w)


def _mlp_kernel(n_ref, h_ref, wu_ref, wd_ref, nf_ref, y_ref, acc_s):
    j = pl.program_id(1)
    hid = jnp.dot(n_ref[...], wu_ref[...], preferred_element_type=F32)
    hid = jnp.square(jnp.maximum(hid, 0.0)).astype(BF16)
    part = jnp.dot(hid, wd_ref[...], preferred_element_type=F32)

    @pl.when(j == 0)
    def _():
        acc_s[...] = h_ref[...] + part

    @pl.when(j > 0)
    def _():
        acc_s[...] += part

    @pl.when(j == pl.num_programs(1) - 1)
    def _():
        h = acc_s[...]
        ms = jnp.mean(h * h, axis=-1, keepdims=True)
        y_ref[...] = (h * lax.rsqrt(ms + EPS)) * nf_ref[...]


def _mlp(n, h, w_up, w_down, norm_f, tm, tf=1024):
    m = n.shape[0]
    return pl.pallas_call(
        _mlp_kernel,
        grid=(m // tm, D_FF // tf),
        in_specs=[
            pl.BlockSpec((tm, D_MODEL), lambda i, j: (i, 0)),
            pl.BlockSpec((tm, D_MODEL), lambda i, j: (i, 0)),
            pl.BlockSpec((D_MODEL, tf), lambda i, j: (0, j)),
            pl.BlockSpec((tf, D_MODEL), lambda i, j: (j, 0)),
            pl.BlockSpec((1, D_MODEL), lambda i, j: (0, 0)),
        ],
        out_specs=pl.BlockSpec((tm, D_MODEL), lambda i, j: (i, 0)),
        out_shape=jax.ShapeDtypeStruct((m, D_MODEL), F32),
        scratch_shapes=[pltpu.VMEM((tm, D_MODEL), F32)],
        compiler_params=pltpu.CompilerParams(
            dimension_semantics=("arbitrary", "arbitrary"), vmem_limit_bytes=VMEM_LIMIT),
        name="mlp",
    )(n, h, w_up, w_down, norm_f)


def _row_tile(m, cap):
    t = min(m, cap)
    while m % t:
        t //= 2
    return t


def kernel(x_prompt, x_sample, state_ssd_conv, state_ssd, state_gdn_conv, state_gdn, meta_tokens, norm_mix_w, w_in, ssd_conv_w, ssd_conv_b, ssd_dt_bias, ssd_a_log, ssd_d, ssd_norm_w, gdn_conv_w, gdn_dt_bias, gdn_a_log, gdn_norm_w, w_out, norm_mlp_w, w_up, w_down, norm_f_w):
    bp, sp_len, _ = x_prompt.shape
    bs, ss_len, _ = x_sample.shape
    assert w_in.shape[0] == 1, "single layer"
    assert sp_len % CHUNK == 0 and ss_len % CHUNK == 0 and N_META <= CHUNK

    wi = w_in[0]
    z_a, xbc_a, dt_a, qkv_b, z_b, beta_b, a_b, gate_a, gate_b = jnp.split(
        wi, [2048, 5120, 5152, 11296, 13344, 13360, 13376, 15424], axis=1)
    xs_w, bc_w = xbc_a[:, :SSD_D_INNER], xbc_a[:, SSD_D_INNER:]
    q_w, k_w, v_w = jnp.split(qkv_b, 3, axis=1)
    w_main = jnp.concatenate([z_a, xs_w, q_w, k_w, v_w, z_b, gate_a, gate_b, bc_w], axis=1).astype(BF16)
    zc = lambda n: jnp.zeros((D_MODEL, n), wi.dtype)
    w_small = jnp.concatenate([dt_a, a_b, zc(LANES - 48), zc(32), beta_b, zc(LANES - 48)], axis=1).astype(BF16)
    w_out_a = w_out[0, :SSD_D_INNER].astype(BF16)
    w_out_b = w_out[0, SSD_D_INNER:].astype(BF16)
    w_up_b = w_up[0].astype(BF16)
    w_down_b = w_down[0].astype(BF16)

    zrow = lambda n: jnp.zeros((1, n), F32)
    sbias = jnp.concatenate([ssd_dt_bias[0][None], gdn_dt_bias[0][None], zrow(LANES - 48)], axis=1)
    alog = jnp.concatenate([ssd_a_log[0][None], gdn_a_log[0][None], zrow(LANES - 48)], axis=1)
    dexp = jnp.repeat(ssd_d[0], SSD_HEAD_DIM)[None]
    prm = (ssd_conv_w[0], ssd_conv_b[0][None], gdn_conv_w[0], sbias, alog, dexp,
           ssd_norm_w[0][None], gdn_norm_w[0][None])
    nmix = norm_mix_w[0][None]
    nmlp = norm_mlp_w[0][None]
    nf = norm_f_w[None]

    def front(x2d, n_seq, seq_len, n_pad, conv_a, ssm_a, conv_b, ssm_b):
        p, ps = _inproj(x2d, nmix, w_main, w_small, _row_tile(x2d.shape[0], 1024))
        return p, _mixer(p, ps, conv_a, ssm_a, conv_b, ssm_b, prm, n_seq, seq_len, n_pad)

    def back(x2d, p, ya, ob):
        h, n = _outproj(x2d, ya, ob, p, w_out_a, w_out_b, nmlp, _row_tile(x2d.shape[0], 256))
        return _mlp(n, h, w_up_b, w_down_b, nf, _row_tile(x2d.shape[0], 512))

    dtp = x_prompt.dtype
    n_pad = CHUNK - N_META
    x_meta = jnp.concatenate([jnp.zeros((n_pad, D_MODEL), dtp), meta_tokens.astype(dtp)], axis=0)
    _, (_, _, cva_m, ssd_m, cvb_m, gdn_m) = front(
        x_meta, 1, CHUNK, n_pad,
        jnp.zeros((1, CONV_K - 1, SSD_CONV_CH), dtp), jnp.zeros((1, SSD_HEADS, SSD_HEAD_DIM, SSD_STATE), dtp),
        jnp.zeros((1, CONV_K - 1, GDN_CONV_CH), dtp), jnp.zeros((1, GDN_HEADS, GDN_HEAD_DIM, GDN_HEAD_DIM), dtp))

    xp2 = x_prompt.reshape(bp * sp_len, D_MODEL)
    p_p, (ya_p, ob_p, cva_p, ssd_p, cvb_p, gdn_p) = front(xp2, bp, sp_len, 0, cva_m, ssd_m, cvb_m, gdn_m)
    y_prompt = back(xp2, p_p, ya_p, ob_p).reshape(bp, sp_len, D_MODEL)

    xs2 = x_sample.reshape(bs * ss_len, D_MODEL)
    p_s, (ya_s, ob_s, cva_s, ssd_s, cvb_s, gdn_s) = front(
        xs2, bs, ss_len, 0, state_ssd_conv[0], state_ssd[0], state_gdn_conv[0], state_gdn[0])
    y_sample = back(xs2, p_s, ya_s, ob_s).reshape(bs, ss_len, D_MODEL)

    return (y_prompt, y_sample, cva_p[None], ssd_p[None], cvb_p[None], gdn_p[None],
            cva_s[None], ssd_s[None], cvb_s[None], gdn_s[None])
```

```python
import functools

import jax
import jax.numpy as jnp
from jax import lax
from jax.experimental import pallas as pl
from jax.experimental.pallas import tpu as pltpu

F32 = jnp.float32
BF16 = jnp.bfloat16
HIGHEST = lax.Precision.HIGHEST

D_MODEL = 2048
CHUNK = 64
N_META = 16
CONV_K = 4
EPS = 1e-6
LOG2E = 1.4426950408889634

SSD_HEADS = 32
SSD_HEAD_DIM = 64
SSD_GROUPS = 4
SSD_STATE = 128
SSD_D_INNER = SSD_HEADS * SSD_HEAD_DIM
SSD_BC = 2 * SSD_GROUPS * SSD_STATE
SSD_CONV_CH = SSD_D_INNER + SSD_BC
SSD_PAIRS = SSD_HEADS // 2

GDN_HEADS = 16
GDN_HEAD_DIM = 128
GDN_WIDTH = GDN_HEADS * GDN_HEAD_DIM
GDN_CONV_CH = 3 * GDN_WIDTH
D_FF = 4 * D_MODEL

LANES = 128
WIDE = 2048
BLK_ZA, BLK_XS, BLK_Q, BLK_K, BLK_V, BLK_ZB, BLK_GA, BLK_GB = range(8)
P_COLS = 8 * WIDE + SSD_BC
BLK_BC = (8 * WIDE) // SSD_BC
SMALL_COLS = 2 * LANES

VMEM_LIMIT = 56 * 1024 * 1024


def _silu(x):
    h = 0.5 * x
    return h * jnp.tanh(h) + h


def _sigmoid(x):
    return 1.0 / (1.0 + jnp.exp(-x))


def _softplus(x):
    return jnp.maximum(x, 0.0) + jnp.log1p(jnp.exp(-jnp.abs(x)))


def _bdot(a, b):
    return jnp.dot(a.astype(BF16), b.astype(BF16), preferred_element_type=F32)


def _bdot_nt(a, b):
    return lax.dot_general(a.astype(BF16), b.astype(BF16), (((1,), (1,)), ((), ())), preferred_element_type=F32)


def _bdot_tn(a, b):
    return lax.dot_general(a.astype(BF16), b.astype(BF16), (((0,), (0,)), ((), ())), preferred_element_type=F32)


def _inproj_kernel(x_ref, nw_ref, w_ref, ws_ref, p_ref, ps_ref, u_s):
    j = pl.program_id(1)

    @pl.when(j == 0)
    def _():
        x = x_ref[...]
        ms = jnp.mean(x * x, axis=-1, keepdims=True)
        u = (x * lax.rsqrt(ms + EPS)) * nw_ref[...]
        u_s[...] = u.astype(BF16)
        ps_ref[...] = jnp.dot(u_s[...], ws_ref[...], preferred_element_type=F32)

    p_ref[...] = jnp.dot(u_s[...], w_ref[...], preferred_element_type=F32)


def _inproj(x2d, norm_w, w_main, w_small, tm, tn=1024):
    m = x2d.shape[0]
    grid = (m // tm, P_COLS // tn)
    return pl.pallas_call(
        _inproj_kernel,
        grid=grid,
        in_specs=[
            pl.BlockSpec((tm, D_MODEL), lambda i, j: (i, 0)),
            pl.BlockSpec((1, D_MODEL), lambda i, j: (0, 0)),
            pl.BlockSpec((D_MODEL, tn), lambda i, j: (0, j)),
            pl.BlockSpec((D_MODEL, SMALL_COLS), lambda i, j: (0, 0)),
        ],
        out_specs=[
            pl.BlockSpec((tm, tn), lambda i, j: (i, j)),
            pl.BlockSpec((tm, SMALL_COLS), lambda i, j: (i, 0)),
        ],
        out_shape=[
            jax.ShapeDtypeStruct((m, P_COLS), F32),
            jax.ShapeDtypeStruct((m, SMALL_COLS), F32),
        ],
        scratch_shapes=[pltpu.VMEM((tm, D_MODEL), BF16)],
        compiler_params=pltpu.CompilerParams(
            dimension_semantics=("arbitrary", "arbitrary"), vmem_limit_bytes=VMEM_LIMIT),
        name="inproj",
    )(x2d, norm_w, w_main, w_small)


def _mixer_kernel(za_ref, xs_ref, q_ref, k_ref, v_ref, zb_ref, bc_ref, sm_ref,
                  cva_ref, ssd_ref, cvb_ref, gdn_ref,
                  cwa_ref, cba_ref, cwb_ref, sbias_ref, alog_ref, dexp_ref, nwa_ref, nwb_ref,
                  ya_ref, ob_ref, cva_o, ssd_o, cvb_o, gdn_o,
                  ha, hb, xs_s, bc_s, q_s, k_s, v_s, ht_s, s_s, y_s, *, n_pad):
    c = pl.program_id(1)
    nc = pl.num_programs(1)
    T = CHUNK
    HIST = CONV_K - 1

    @pl.when(c == 0)
    def _():
        ha[...] = jnp.concatenate([jnp.zeros((8 - HIST, SSD_CONV_CH), F32), cva_ref[...]], axis=0)
        hb[...] = jnp.concatenate([jnp.zeros((8 - HIST, GDN_CONV_CH), F32), cvb_ref[...]], axis=0)
        for j in range(SSD_PAIRS):
            ht_s[:, j * LANES:(j + 1) * LANES] = ssd_ref[j].T
        s_s[...] = gdn_ref[...]

    @pl.when(c == nc - 1)
    def _():
        cva_o[:, 0:SSD_D_INNER] = xs_ref[T - HIST:T, :]
        cva_o[:, SSD_D_INNER:SSD_CONV_CH] = bc_ref[T - HIST:T, :]
        cvb_o[:, 0:GDN_WIDTH] = q_ref[T - HIST:T, :]
        cvb_o[:, GDN_WIDTH:2 * GDN_WIDTH] = k_ref[T - HIST:T, :]
        cvb_o[:, 2 * GDN_WIDTH:3 * GDN_WIDTH] = v_ref[T - HIST:T, :]

    def conv_slab(in_ref, i_lo, hist, w_ref, b_ref, c_lo, out_ref, o_lo, wd=512):
        ci = slice(i_lo, i_lo + wd)
        cc = slice(c_lo, c_lo + wd)
        x = in_ref[:, ci]
        full = jnp.concatenate([hist[0:8, cc], x], axis=0)
        w = [w_ref[j:j + 1, cc] for j in range(CONV_K)]
        f1 = pltpu.roll(full, 1, axis=0)
        pair_b = full * w[1] + f1 * w[0]
        y = x * w[3] + f1[8:8 + T] * w[2] + pltpu.roll(pair_b, 2, axis=0)[8:8 + T]
        if b_ref is not None:
            y = y + b_ref[:, cc]
        out_ref[:, o_lo:o_lo + wd] = _silu(y)
        hist[0:8, cc] = x[T - 8:T]

    def conv_tasks(in_ref, hist, w_ref, b_ref, c_base, out_ref, width, wd=512):
        return [functools.partial(conv_slab, in_ref, s * wd, hist, w_ref, b_ref, c_base + s * wd, out_ref, s * wd, wd)
                for s in range(width // wd)]

    lane = lax.broadcasted_iota(jnp.int32, (T, LANES), 1)
    row = lax.broadcasted_iota(jnp.int32, (T, LANES), 0)
    sp = _softplus(sm_ref[:, 0:LANES] + sbias_ref[...])
    sg = _sigmoid(sm_ref[:, LANES:2 * LANES])
    neg_a = -jnp.exp(alog_ref[...])
    used = lane < SSD_HEADS + GDN_HEADS
    gmat = jnp.where(used, sp * neg_a, 0.0)
    vmat = jnp.where(lane < SSD_HEADS, sp, jnp.where(used, sg, 0.0))
    if n_pad:
        live = row >= n_pad
        gmat = jnp.where(live, gmat, 0.0)
        vmat = jnp.where(live, vmat, 0.0)
    tr = lax.broadcasted_iota(jnp.int32, (T, T), 0)
    tc = lax.broadcasted_iota(jnp.int32, (T, T), 1)
    tril = jnp.where(tr >= tc, 1.0, 0.0).astype(F32)
    cum = jnp.dot(tril, gmat, precision=HIGHEST, preferred_element_type=F32) * LOG2E
    zpad = jnp.zeros((T, LANES), F32)
    cumt = jnp.concatenate([cum, zpad], axis=0).T
    vmt = jnp.concatenate([vmat, zpad], axis=0).T
    last_all = cum[T - 1:T, :]
    wsm = jnp.exp2(last_all - cum) * vmat

    def lane_bcast(m, col):
        return jnp.broadcast_to(m[:, col:col + 1], (T, LANES))

    def pair_cols(mt, r0, r1):
        return jnp.concatenate([mt[r0:r0 + 1, 0:T], mt[r1:r1 + 1, 0:T]], axis=1)

    lo_half = lane < T
    scol = lane & (T - 1)
    incl2 = row >= scol
    strict2 = row > scol
    mask_lo = jnp.where(lo_half, 1.0, 0.0).astype(BF16)
    mask_hi = jnp.where(lo_half, 0.0, 1.0).astype(BF16)

    def blockdiag(yp):
        yb = yp.astype(BF16)
        return jnp.concatenate([yb * mask_lo, yb * mask_hi], axis=0)

    gw = SSD_D_INNER // SSD_GROUPS
    ppg = SSD_PAIRS // SSD_GROUPS

    def ssd_group_tasks(g):
        st = {}
        gs = slice(g * gw, (g + 1) * gw)

        def prologue():
            bm = bc_s[:, g * SSD_STATE:(g + 1) * SSD_STATE]
            cm = bc_s[:, (SSD_GROUPS + g) * SSD_STATE:(SSD_GROUPS + g + 1) * SSD_STATE]
            st["bmb"] = bm.astype(BF16)
            st["cb2"] = _bdot_nt(cm, jnp.concatenate([st["bmb"], st["bmb"]], axis=0))
            st["yoff"] = _bdot(cm, ht_s[:, gs])
            st["xw"], st["elast"] = [], []

        def pair(jj):
            j = g * ppg + jj
            r0, r1 = 2 * j, 2 * j + 1
            sl = slice(j * LANES, (j + 1) * LANES)
            rowp = jnp.where(lo_half, lane_bcast(cum, r0), lane_bcast(cum, r1))
            wb = jnp.where(lo_half, lane_bcast(wsm, r0), lane_bcast(wsm, r1))
            colp = pair_cols(cumt, r0, r1)
            dtrow = pair_cols(vmt, r0, r1)
            lmat = jnp.exp2(jnp.where(incl2, rowp - colp, -jnp.inf))
            mp = st["cb2"] * lmat * dtrow
            xsp = xs_s[:, sl]
            e1 = jnp.exp2(rowp)
            y = (_bdot(mp, blockdiag(xsp)) + st["yoff"][:, jj * LANES:(jj + 1) * LANES] * e1
                 + xsp * dexp_ref[:, sl])
            y_s[:, sl] = y * _silu(za_ref[:, sl])
            st["xw"].append((xsp * wb).astype(BF16))
            st["elast"].append(e1[T - 1:T, :])

        def epilogue():
            xw = jnp.concatenate(st["xw"], axis=1)
            elast = jnp.concatenate(st["elast"], axis=1)
            ht_s[:, gs] = ht_s[:, gs] * elast + _bdot_tn(st["bmb"], xw)
            yg = y_s[:, gs]
            ms = jnp.mean(yg * yg, axis=-1, keepdims=True)
            ya_ref[:, gs] = ((yg * lax.rsqrt(ms + EPS)) * nwa_ref[:, gs]).astype(BF16)

        return [prologue] + [functools.partial(pair, jj) for jj in range(ppg)] + [epilogue]

    fillers = (conv_tasks(v_ref, hb, cwb_ref, None, 2 * GDN_WIDTH, v_s, GDN_WIDTH)
               + conv_tasks(xs_ref, ha, cwa_ref, cba_ref, 0, xs_s, SSD_D_INNER)
               + conv_tasks(bc_ref, ha, cwa_ref, cba_ref, SSD_D_INNER, bc_s, SSD_BC))
    for g in range(SSD_GROUPS):
        fillers += ssd_group_tasks(g)
    n_waves = 11
    per_wave = -(-len(fillers) // n_waves)

    def fill():
        for _ in range(min(per_wave, len(fillers))):
            fillers.pop(0)()

    for t in conv_tasks(q_ref, hb, cwb_ref, None, 0, q_s, GDN_WIDTH) + conv_tasks(k_ref, hb, cwb_ref, None, GDN_WIDTH, k_s, GDN_WIDTH):
        t()
    q_scale = GDN_HEAD_DIM ** -0.5
    eye2 = jnp.where(row == scol, 1.0, 0.0).astype(F32)
    blk16 = (row // 16) == (scol // 16)
    blk32 = (row // 32) == (scol // 32)
    zb = jnp.zeros((T, LANES), BF16)
    zb2 = jnp.zeros((T, 2 * LANES), BF16)
    npair = GDN_HEADS // 2

    qn, kn, gcb, betab, eg = [], [], [], [], []
    for h in range(GDN_HEADS):
        sl = slice(h * LANES, (h + 1) * LANES)
        qh = q_s[:, sl]
        kh = k_s[:, sl]
        qn.append(qh * (lax.rsqrt(jnp.sum(qh * qh, axis=-1, keepdims=True) + EPS) * q_scale))
        kn.append(kh * lax.rsqrt(jnp.sum(kh * kh, axis=-1, keepdims=True) + EPS))
        gcb.append(lane_bcast(cum, SSD_HEADS + h))
        betab.append(lane_bcast(vmat, SSD_HEADS + h))
        eg.append(jnp.exp2(gcb[-1]))

    decayp, qkp, ap = [], [], []
    for i in range(npair):
        h0, h1 = 2 * i, 2 * i + 1
        growp = jnp.where(lo_half, gcb[h0], gcb[h1])
        betap = jnp.where(lo_half, betab[h0], betab[h1])
        gcolp = pair_cols(cumt, SSD_HEADS + h0, SSD_HEADS + h1)
        decayp.append(jnp.exp2(jnp.where(incl2, growp - gcolp, -jnp.inf)))
        knb = [kn[h0].astype(BF16), kn[h1].astype(BF16)]
        lhs = jnp.concatenate([jnp.concatenate([qn[h0].astype(BF16), qn[h1].astype(BF16)], axis=1),
                               jnp.concatenate(knb, axis=1)], axis=0)
        rhst = jnp.concatenate([jnp.concatenate([knb[0], zb], axis=1),
                                jnp.concatenate([zb, knb[1]], axis=1)], axis=0)
        qkkk = _bdot_nt(lhs, rhst)
        qkp.append(qkkk[0:T])
        ap.append(jnp.where(strict2, betap * qkkk[T:2 * T] * decayp[i], 0.0))
    fill()

    d = [jnp.where(blk16, a, 0.0) for a in ap]
    pw = [_bdot(di, blockdiag(di)) for di in d]
    x = [eye2 - di for di in d]
    fill()
    for lvl in range(3):
        bdp = [blockdiag(p_) for p_ in pw]
        if lvl < 2:
            r = [_bdot(jnp.concatenate([x[i], pw[i]], axis=0), bdp[i]) for i in range(npair)]
            x = [x[i] + r[i][0:T] for i in range(npair)]
            pw = [r[i][T:2 * T] for i in range(npair)]
        else:
            x = [x[i] + _bdot(x[i], bdp[i]) for i in range(npair)]
        fill()
    for lvl in range(2):
        if lvl == 0:
            off = [jnp.where(blk32 & ~blk16, a, 0.0) for a in ap]
        else:
            off = [jnp.where(blk32, 0.0, a) for a in ap]
        y = [_bdot(x[i], blockdiag(off[i])) for i in range(npair)]
        fill()
        x = [x[i] - _bdot(y[i], blockdiag(x[i])) for i in range(npair)]
        fill()

    uw = []
    for i in range(npair):
        rhs = [jnp.concatenate([(v_s[:, h * LANES:(h + 1) * LANES] * betab[h]).astype(BF16),
                                (kn[h] * (betab[h] * eg[h])).astype(BF16)], axis=1)
               for h in (2 * i, 2 * i + 1)]
        bdr = jnp.concatenate([jnp.concatenate([rhs[0], zb2], axis=1),
                               jnp.concatenate([zb2, rhs[1]], axis=1)], axis=0)
        uw.append(_bdot(x[i], bdr))
    fill()

    wv, qs = [], []
    for h in range(GDN_HEADS):
        i, e = divmod(h, 2)
        u = uw[i][:, e * 2 * LANES:e * 2 * LANES + LANES]
        w = uw[i][:, e * 2 * LANES + LANES:(e + 1) * 2 * LANES]
        wq = _bdot(jnp.concatenate([w, qn[h] * eg[h]], axis=0), s_s[h])
        wv.append((u - wq[0:T]).astype(BF16))
        qs.append(wq[T:2 * T])
    fill()

    oo = []
    for i in range(npair):
        bdw = jnp.concatenate([jnp.concatenate([wv[2 * i], zb], axis=1),
                               jnp.concatenate([zb, wv[2 * i + 1]], axis=1)], axis=0)
        oo.append(_bdot(qkp[i] * decayp[i], bdw))
    fill()
    while fillers:
        fill()
    for h in range(GDN_HEADS):
        i, e = divmod(h, 2)
        sl = slice(h * LANES, (h + 1) * LANES)
        oh = qs[h] + oo[i][:, e * LANES:(e + 1) * LANES]
        last = gcb[h][T - 1:T, :]
        kdec = kn[h] * jnp.exp2(last - gcb[h])
        s_s[h] = s_s[h] * eg[h][T - 1:T, :] + _bdot_tn(kdec, wv[h])
        ms = jnp.mean(oh * oh, axis=-1, keepdims=True)
        on = (oh * lax.rsqrt(ms + EPS)) * nwb_ref[...]
        ob_ref[:, sl] = (on * _silu(zb_ref[:, sl])).astype(BF16)

    @pl.when(c == nc - 1)
    def _():
        for j in range(SSD_PAIRS):
            ssd_o[j] = ht_s[:, j * LANES:(j + 1) * LANES].T
        gdn_o[...] = s_s[...]


def _mixer(p, ps, conv_a, ssm_a, conv_b, ssm_b, prm, n_seq, seq_len, n_pad):
    nc = seq_len // CHUNK
    m = n_seq * seq_len
    T = CHUNK

    def rows(blk):
        return lambda b, c: (b * nc + c, blk)

    def state_map(arr):
        shared = arr.shape[0] == 1
        nd = arr.ndim
        return lambda b, c: ((0 if shared else b),) + (0,) * (nd - 1)

    def full(arr):
        nd = arr.ndim
        return pl.BlockSpec(arr.shape, lambda b, c: (0,) * nd)

    ssm_a4 = ssm_a.reshape(ssm_a.shape[0], SSD_PAIRS, 2 * SSD_HEAD_DIM, SSD_STATE)
    in_specs = [
        pl.BlockSpec((T, WIDE), rows(BLK_ZA)),
        pl.BlockSpec((T, WIDE), rows(BLK_XS)),
        pl.BlockSpec((T, WIDE), rows(BLK_Q)),
        pl.BlockSpec((T, WIDE), rows(BLK_K)),
        pl.BlockSpec((T, WIDE), rows(BLK_V)),
        pl.BlockSpec((T, WIDE), rows(BLK_ZB)),
        pl.BlockSpec((T, SSD_BC), rows(BLK_BC)),
        pl.BlockSpec((T, SMALL_COLS), rows(0)),
        pl.BlockSpec((None, CONV_K - 1, SSD_CONV_CH), state_map(conv_a)),
        pl.BlockSpec((None, SSD_PAIRS, 2 * SSD_HEAD_DIM, SSD_STATE), state_map(ssm_a4)),
        pl.BlockSpec((None, CONV_K - 1, GDN_CONV_CH), state_map(conv_b)),
        pl.BlockSpec((None, GDN_HEADS, GDN_HEAD_DIM, GDN_HEAD_DIM), state_map(ssm_b)),
    ] + [full(a) for a in prm]
    out_specs = [
        pl.BlockSpec((T, SSD_D_INNER), lambda b, c: (b * nc + c, 0)),
        pl.BlockSpec((T, GDN_WIDTH), lambda b, c: (b * nc + c, 0)),
        pl.BlockSpec((None, CONV_K - 1, SSD_CONV_CH), lambda b, c: (b, 0, 0)),
        pl.BlockSpec((None, SSD_PAIRS, 2 * SSD_HEAD_DIM, SSD_STATE), lambda b, c: (b, 0, 0, 0)),
        pl.BlockSpec((None, CONV_K - 1, GDN_CONV_CH), lambda b, c: (b, 0, 0)),
        pl.BlockSpec((None, GDN_HEADS, GDN_HEAD_DIM, GDN_HEAD_DIM), lambda b, c: (b, 0, 0, 0)),
    ]
    out_shape = [
        jax.ShapeDtypeStruct((m, SSD_D_INNER), BF16),
        jax.ShapeDtypeStruct((m, GDN_WIDTH), BF16),
        jax.ShapeDtypeStruct((n_seq, CONV_K - 1, SSD_CONV_CH), F32),
        jax.ShapeDtypeStruct((n_seq, SSD_PAIRS, 2 * SSD_HEAD_DIM, SSD_STATE), F32),
        jax.ShapeDtypeStruct((n_seq, CONV_K - 1, GDN_CONV_CH), F32),
        jax.ShapeDtypeStruct((n_seq, GDN_HEADS, GDN_HEAD_DIM, GDN_HEAD_DIM), F32),
    ]
    scratch = [
        pltpu.VMEM((8, SSD_CONV_CH), F32),
        pltpu.VMEM((8, GDN_CONV_CH), F32),
        pltpu.VMEM((T, SSD_D_INNER), F32),
        pltpu.VMEM((T, SSD_BC), F32),
        pltpu.VMEM((T, GDN_WIDTH), F32),
        pltpu.VMEM((T, GDN_WIDTH), F32),
        pltpu.VMEM((T, GDN_WIDTH), F32),
        pltpu.VMEM((SSD_STATE, SSD_D_INNER), F32),
        pltpu.VMEM((GDN_HEADS, GDN_HEAD_DIM, GDN_HEAD_DIM), F32),
        pltpu.VMEM((T, SSD_D_INNER), F32),
    ]
    outs = pl.pallas_call(
        functools.partial(_mixer_kernel, n_pad=n_pad),
        grid=(n_seq, nc),
        in_specs=in_specs,
        out_specs=out_specs,
        out_shape=out_shape,
        scratch_shapes=scratch,
        compiler_params=pltpu.CompilerParams(
            dimension_semantics=("arbitrary", "arbitrary"), vmem_limit_bytes=VMEM_LIMIT),
        name="mixer",
    )(p, p, p, p, p, p, p, ps, conv_a, ssm_a4, conv_b, ssm_b, *prm)
    ya, ob, cva, ssd, cvb, gdn = outs
    return ya, ob, cva, ssd.reshape(n_seq, SSD_HEADS, SSD_HEAD_DIM, SSD_STATE), cvb, gdn


def _outproj_kernel(x_ref, ya_ref, ob_ref, ga_ref, gb_ref, wa_ref, wb_ref, nw_ref, h_ref, n_ref):
    ma = jnp.dot(ya_ref[...], wa_ref[...], preferred_element_type=F32)
    mb = jnp.dot(ob_ref[...], wb_ref[...], preferred_element_type=F32)
    h = x_ref[...] + _sigmoid(ga_ref[...]) * ma + _sigmoid(gb_ref[...]) * mb
    h_ref[...] = h
    ms = jnp.mean(h * h, axis=-1, keepdims=True)
    n_ref[...] = ((h * lax.rsqrt(ms + EPS)) * nw_ref[...]).astype(BF16)


def _outproj(x2d, ya, ob, p, w_out_a, w_out_b, norm_w, tm):
    m = x2d.shape[0]
    const = lambda i: (0, 0)
    return pl.pallas_call(
        _outproj_kernel,
        grid=(m // tm,),
        in_specs=[
            pl.BlockSpec((tm, D_MODEL), lambda i: (i, 0)),
            pl.BlockSpec((tm, SSD_D_INNER), lambda i: (i, 0)),
            pl.BlockSpec((tm, GDN_WIDTH), lambda i: (i, 0)),
            pl.BlockSpec((tm, WIDE), lambda i: (i, BLK_GA)),
            pl.BlockSpec((tm, WIDE), lambda i: (i, BLK_GB)),
            pl.BlockSpec((SSD_D_INNER, D_MODEL), const, pipeline_mode=pl.Buffered(1)),
            pl.BlockSpec((GDN_WIDTH, D_MODEL), const, pipeline_mode=pl.Buffered(1)),
            pl.BlockSpec((1, D_MODEL), const),
        ],
        out_specs=[
            pl.BlockSpec((tm, D_MODEL), lambda i: (i, 0)),
            pl.BlockSpec((tm, D_MODEL), lambda i: (i, 0)),
        ],
        out_shape=[
            jax.ShapeDtypeStruct((m, D_MODEL), F32),
            jax.ShapeDtypeStruct((m, D_MODEL), BF16),
        ],
        compiler_params=pltpu.CompilerParams(
            dimension_semantics=("arbitrary",), vmem_limit_bytes=VMEM_LIMIT),
        name="outproj",
    )(x2d, ya, ob, p, p, w_out_a, w_out_b, norm_w)


def _mlp_kernel(n_ref, h_ref, wu_ref, wd_ref, nf_ref, y_ref, acc_s):
    j = pl.program_id(1)
    hid = jnp.dot(n_ref[...], wu_ref[...], preferred_element_type=F32)
    hid = jnp.square(jnp.maximum(hid, 0.0)).astype(BF16)
    part = jnp.dot(hid, wd_ref[...], preferred_element_type=F32)

    @pl.when(j == 0)
    def _():
        acc_s[...] = h_ref[...] + part

    @pl.when(j > 0)
    def _():
        acc_s[...] += part

    @pl.when(j == pl.num_programs(1) - 1)
    def _():
        h = acc_s[...]
        ms = jnp.mean(h * h, axis=-1, keepdims=True)
        y_ref[...] = (h * lax.rsqrt(ms + EPS)) * nf_ref[...]


def _mlp(n, h, w_up, w_down, norm_f, tm, tf=1024):
    m = n.shape[0]
    return pl.pallas_call(
        _mlp_kernel,
        grid=(m // tm, D_FF // tf),
        in_specs=[
            pl.BlockSpec((tm, D_MODEL), lambda i, j: (i, 0)),
            pl.BlockSpec((tm, D_MODEL), lambda i, j: (i, 0)),
            pl.BlockSpec((D_MODEL, tf), lambda i, j: (0, j)),
            pl.BlockSpec((tf, D_MODEL), lambda i, j: (j, 0)),
            pl.BlockSpec((1, D_MODEL), lambda i, j: (0, 0)),
        ],
        out_specs=pl.BlockSpec((tm, D_MODEL), lambda i, j: (i, 0)),
        out_shape=jax.ShapeDtypeStruct((m, D_MODEL), F32),
        scratch_shapes=[pltpu.VMEM((tm, D_MODEL), F32)],
        compiler_params=pltpu.CompilerParams(
            dimension_semantics=("arbitrary", "arbitrary"), vmem_limit_bytes=VMEM_LIMIT),
        name="mlp",
    )(n, h, w_up, w_down, norm_f)


def _row_tile(m, cap):
    t = min(m, cap)
    while m % t:
        t //= 2
    return t


def kernel(x_prompt, x_sample, state_ssd_conv, state_ssd, state_gdn_conv, state_gdn, meta_tokens, norm_mix_w, w_in, ssd_conv_w, ssd_conv_b, ssd_dt_bias, ssd_a_log, ssd_d, ssd_norm_w, gdn_conv_w, gdn_dt_bias, gdn_a_log, gdn_norm_w, w_out, norm_mlp_w, w_up, w_down, norm_f_w):
    bp, sp_len, _ = x_prompt.shape
    bs, ss_len, _ = x_sample.shape
    assert w_in.shape[0] == 1, "single layer"
    assert sp_len % CHUNK == 0 and ss_len % CHUNK == 0 and N_META <= CHUNK

    wi = w_in[0]
    z_a, xbc_a, dt_a, qkv_b, z_b, beta_b, a_b, gate_a, gate_b = jnp.split(
        wi, [2048, 5120, 5152, 11296, 13344, 13360, 13376, 15424], axis=1)
    xs_w, bc_w = xbc_a[:, :SSD_D_INNER], xbc_a[:, SSD_D_INNER:]
    q_w, k_w, v_w = jnp.split(qkv_b, 3, axis=1)
    w_main = jnp.concatenate([z_a, xs_w, q_w, k_w, v_w, z_b, gate_a, gate_b, bc_w], axis=1).astype(BF16)
    zc = lambda n: jnp.zeros((D_MODEL, n), wi.dtype)
    w_small = jnp.concatenate([dt_a, a_b, zc(LANES - 48), zc(32), beta_b, zc(LANES - 48)], axis=1).astype(BF16)
    w_out_a = w_out[0, :SSD_D_INNER].astype(BF16)
    w_out_b = w_out[0, SSD_D_INNER:].astype(BF16)
    w_up_b = w_up[0].astype(BF16)
    w_down_b = w_down[0].astype(BF16)

    zrow = lambda n: jnp.zeros((1, n), F32)
    sbias = jnp.concatenate([ssd_dt_bias[0][None], gdn_dt_bias[0][None], zrow(LANES - 48)], axis=1)
    alog = jnp.concatenate([ssd_a_log[0][None], gdn_a_log[0][None], zrow(LANES - 48)], axis=1)
    dexp = jnp.repeat(ssd_d[0], SSD_HEAD_DIM)[None]
    prm = (ssd_conv_w[0], ssd_conv_b[0][None], gdn_conv_w[0], sbias, alog, dexp,
           ssd_norm_w[0][None], gdn_norm_w[0][None])
    nmix = norm_mix_w[0][None]
    nmlp = norm_mlp_w[0][None]
    nf = norm_f_w[None]

    def front(x2d, n_seq, seq_len, n_pad, conv_a, ssm_a, conv_b, ssm_b):
        p, ps = _inproj(x2d, nmix, w_main, w_small, _row_tile(x2d.shape[0], 1024))
        return p, _mixer(p, ps, conv_a, ssm_a, conv_b, ssm_b, prm, n_seq, seq_len, n_pad)

    def back(x2d, p, ya, ob):
        h, n = _outproj(x2d, ya, ob, p, w_out_a, w_out_b, nmlp, _row_tile(x2d.shape[0], 256))
        return _mlp(n, h, w_up_b, w_down_b, nf, _row_tile(x2d.shape[0], 512))

    dtp = x_prompt.dtype
    n_pad = CHUNK - N_META
    x_meta = jnp.concatenate([jnp.zeros((n_pad, D_MODEL), dtp), meta_tokens.astype(dtp)], axis=0)
    _, (_, _, cva_m, ssd_m, cvb_m, gdn_m) = front(
        x_meta, 1, CHUNK, n_pad,
        jnp.zeros((1, CONV_K - 1, SSD_CONV_CH), dtp), jnp.zeros((1, SSD_HEADS, SSD_HEAD_DIM, SSD_STATE), dtp),
        jnp.zeros((1, CONV_K - 1, GDN_CONV_CH), dtp), jnp.zeros((1, GDN_HEADS, GDN_HEAD_DIM, GDN_HEAD_DIM), dtp))

    xp2 = x_prompt.reshape(bp * sp_len, D_MODEL)
    p_p, (ya_p, ob_p, cva_p, ssd_p, cvb_p, gdn_p) = front(xp2, bp, sp_len, 0, cva_m, ssd_m, cvb_m, gdn_m)
    y_prompt = back(xp2, p_p, ya_p, ob_p).reshape(bp, sp_len, D_MODEL)

    xs2 = x_sample.reshape(bs * ss_len, D_MODEL)
    p_s, (ya_s, ob_s, cva_s, ssd_s, cvb_s, gdn_s) = front(
        xs2, bs, ss_len, 0, state_ssd_conv[0], state_ssd[0], state_gdn_conv[0], state_gdn[0])
    y_sample = back(xs2, p_s, ya_s, ob_s).reshape(bs, ss_len, D_MODEL)

    return (y_prompt, y_sample, cva_p[None], ssd_p[None], cvb_p[None], gdn_p[None],
            cva_s[None], ssd_s[None], cvb_s[None], gdn_s[None])
```

```python
import functools

import jax
import jax.numpy as jnp
from jax import lax
from jax.experimental import pallas as pl
from jax.experimental.pallas import tpu as pltpu

F32 = jnp.float32
BF16 = jnp.bfloat16
HIGHEST = lax.Precision.HIGHEST

D_MODEL = 2048
CHUNK = 64
N_META = 16
CONV_K = 4
EPS = 1e-6
LOG2E = 1.4426950408889634

SSD_HEADS = 32
SSD_HEAD_DIM = 64
SSD_GROUPS = 4
SSD_STATE = 128
SSD_D_INNER = SSD_HEADS * SSD_HEAD_DIM
SSD_BC = 2 * SSD_GROUPS * SSD_STATE
SSD_CONV_CH = SSD_D_INNER + SSD_BC
SSD_PAIRS = SSD_HEADS // 2

GDN_HEADS = 16
GDN_HEAD_DIM = 128
GDN_WIDTH = GDN_HEADS * GDN_HEAD_DIM
GDN_CONV_CH = 3 * GDN_WIDTH
D_FF = 4 * D_MODEL

LANES = 128
WIDE = 2048
BLK_ZA, BLK_XS, BLK_Q, BLK_K, BLK_V, BLK_ZB, BLK_GA, BLK_GB = range(8)
P_COLS = 8 * WIDE + SSD_BC
BLK_BC = (8 * WIDE) // SSD_BC
SMALL_COLS = 2 * LANES

VMEM_LIMIT = 56 * 1024 * 1024


def _silu(x):
    h = 0.5 * x
    return h * jnp.tanh(h) + h


def _sigmoid(x):
    return 1.0 / (1.0 + jnp.exp(-x))


def _softplus(x):
    return jnp.maximum(x, 0.0) + jnp.log1p(jnp.exp(-jnp.abs(x)))


def _bdot(a, b):
    return jnp.dot(a.astype(BF16), b.astype(BF16), preferred_element_type=F32)


def _bdot_nt(a, b):
    return lax.dot_general(a.astype(BF16), b.astype(BF16), (((1,), (1,)), ((), ())), preferred_element_type=F32)


def _bdot_tn(a, b):
    return lax.dot_general(a.astype(BF16), b.astype(BF16), (((0,), (0,)), ((), ())), preferred_element_type=F32)


def _norm_kernel(x_ref, nw_ref, u_ref):
    x = x_ref[...]
    ms = jnp.mean(x * x, axis=-1, keepdims=True)
    u_ref[...] = ((x * lax.rsqrt(ms + EPS)) * nw_ref[...]).astype(BF16)


def _norm(x2d, norm_w, tm):
    m = x2d.shape[0]
    return pl.pallas_call(
        _norm_kernel,
        grid=(m // tm,),
        in_specs=[pl.BlockSpec((tm, D_MODEL), lambda i: (i, 0)), pl.BlockSpec((1, D_MODEL), lambda i: (0, 0))],
        out_specs=pl.BlockSpec((tm, D_MODEL), lambda i: (i, 0)),
        out_shape=jax.ShapeDtypeStruct((m, D_MODEL), BF16),
        compiler_params=pltpu.CompilerParams(dimension_semantics=("arbitrary",), vmem_limit_bytes=VMEM_LIMIT),
        name="norm",
    )(x2d, norm_w)


def _inproj_kernel(u_ref, w_ref, ws_ref, p_ref, ps_ref):
    @pl.when(pl.program_id(1) == 0)
    def _():
        ps_ref[...] = jnp.dot(u_ref[...], ws_ref[...], preferred_element_type=F32)

    p_ref[...] = jnp.dot(u_ref[...], w_ref[...], preferred_element_type=F32)


def _inproj(u, w_main, w_small, tm, tn=512):
    m = u.shape[0]
    return pl.pallas_call(
        _inproj_kernel,
        grid=(m // tm, P_COLS // tn),
        in_specs=[
            pl.BlockSpec((tm, D_MODEL), lambda i, j: (i, 0)),
            pl.BlockSpec((D_MODEL, tn), lambda i, j: (0, j)),
            pl.BlockSpec((D_MODEL, SMALL_COLS), lambda i, j: (0, 0)),
        ],
        out_specs=[
            pl.BlockSpec((tm, tn), lambda i, j: (i, j)),
            pl.BlockSpec((tm, SMALL_COLS), lambda i, j: (i, 0)),
        ],
        out_shape=[
            jax.ShapeDtypeStruct((m, P_COLS), F32),
            jax.ShapeDtypeStruct((m, SMALL_COLS), F32),
        ],
        compiler_params=pltpu.CompilerParams(
            dimension_semantics=("arbitrary", "arbitrary"), vmem_limit_bytes=VMEM_LIMIT),
        name="inproj",
    )(u, w_main, w_small)


def _mixer_kernel(za_ref, xs_ref, q_ref, k_ref, v_ref, zb_ref, bc_ref, sm_ref,
                  cva_ref, ssd_ref, cvb_ref, gdn_ref,
                  cwa_ref, cba_ref, cwb_ref, sbias_ref, alog_ref, dexp_ref, nwa_ref, nwb_ref,
                  ya_ref, ob_ref, cva_o, ssd_o, cvb_o, gdn_o,
                  ha, hb, xs_s, bc_s, q_s, k_s, v_s, ht_s, s_s, y_s, *, n_pad):
    c = pl.program_id(1)
    nc = pl.num_programs(1)
    T = CHUNK
    HIST = CONV_K - 1

    @pl.when(c == 0)
    def _():
        ha[...] = jnp.concatenate([jnp.zeros((8 - HIST, SSD_CONV_CH), F32), cva_ref[...]], axis=0)
        hb[...] = jnp.concatenate([jnp.zeros((8 - HIST, GDN_CONV_CH), F32), cvb_ref[...]], axis=0)
        for j in range(SSD_PAIRS):
            ht_s[:, j * LANES:(j + 1) * LANES] = ssd_ref[j].T
        s_s[...] = gdn_ref[...]

    @pl.when(c == nc - 1)
    def _():
        cva_o[:, 0:SSD_D_INNER] = xs_ref[T - HIST:T, :]
        cva_o[:, SSD_D_INNER:SSD_CONV_CH] = bc_ref[T - HIST:T, :]
        cvb_o[:, 0:GDN_WIDTH] = q_ref[T - HIST:T, :]
        cvb_o[:, GDN_WIDTH:2 * GDN_WIDTH] = k_ref[T - HIST:T, :]
        cvb_o[:, 2 * GDN_WIDTH:3 * GDN_WIDTH] = v_ref[T - HIST:T, :]

    def conv_slab(in_ref, i_lo, hist, w_ref, b_ref, c_lo, out_ref, o_lo, wd=512):
        ci = slice(i_lo, i_lo + wd)
        cc = slice(c_lo, c_lo + wd)
        x = in_ref[:, ci]
        full = jnp.concatenate([hist[0:8, cc], x], axis=0)
        w = [w_ref[j:j + 1, cc] for j in range(CONV_K)]
        f1 = pltpu.roll(full, 1, axis=0)
        pair_b = full * w[1] + f1 * w[0]
        y = x * w[3] + f1[8:8 + T] * w[2] + pltpu.roll(pair_b, 2, axis=0)[8:8 + T]
        if b_ref is not None:
            y = y + b_ref[:, cc]
        out_ref[:, o_lo:o_lo + wd] = _silu(y)
        hist[0:8, cc] = x[T - 8:T]

    def conv_tasks(in_ref, hist, w_ref, b_ref, c_base, out_ref, width, wd=512):
        return [functools.partial(conv_slab, in_ref, s * wd, hist, w_ref, b_ref, c_base + s * wd, out_ref, s * wd, wd)
                for s in range(width // wd)]

    lane = lax.broadcasted_iota(jnp.int32, (T, LANES), 1)
    row = lax.broadcasted_iota(jnp.int32, (T, LANES), 0)
    sp = _softplus(sm_ref[:, 0:LANES] + sbias_ref[...])
    sg = _sigmoid(sm_ref[:, LANES:2 * LANES])
    neg_a = -jnp.exp(alog_ref[...])
    used = lane < SSD_HEADS + GDN_HEADS
    gmat = jnp.where(used, sp * neg_a, 0.0)
    vmat = jnp.where(lane < SSD_HEADS, sp, jnp.where(used, sg, 0.0))
    if n_pad:
        live = row >= n_pad
        gmat = jnp.where(live, gmat, 0.0)
        vmat = jnp.where(live, vmat, 0.0)
    tr = lax.broadcasted_iota(jnp.int32, (T, T), 0)
    tc = lax.broadcasted_iota(jnp.int32, (T, T), 1)
    tril = jnp.where(tr >= tc, 1.0, 0.0).astype(F32)
    cum = jnp.dot(tril, gmat, precision=HIGHEST, preferred_element_type=F32) * LOG2E
    zpad = jnp.zeros((T, LANES), F32)
    cumt = jnp.concatenate([cum, zpad], axis=0).T
    vmt = jnp.concatenate([vmat, zpad], axis=0).T
    last_all = cum[T - 1:T, :]
    wsm = jnp.exp2(last_all - cum) * vmat

    def lane_bcast(m, col):
        return jnp.broadcast_to(m[:, col:col + 1], (T, LANES))

    def pair_cols(mt, r0, r1):
        return jnp.concatenate([mt[r0:r0 + 1, 0:T], mt[r1:r1 + 1, 0:T]], axis=1)

    lo_half = lane < T
    scol = lane & (T - 1)
    incl2 = row >= scol
    strict2 = row > scol
    mask_lo = jnp.where(lo_half, 1.0, 0.0).astype(BF16)
    mask_hi = jnp.where(lo_half, 0.0, 1.0).astype(BF16)

    def blockdiag(yp):
        yb = yp.astype(BF16)
        return jnp.concatenate([yb * mask_lo, yb * mask_hi], axis=0)

    gw = SSD_D_INNER // SSD_GROUPS
    ppg = SSD_PAIRS // SSD_GROUPS

    def ssd_group_tasks(g):
        st = {}
        gs = slice(g * gw, (g + 1) * gw)

        def prologue():
            bm = bc_s[:, g * SSD_STATE:(g + 1) * SSD_STATE]
            cm = bc_s[:, (SSD_GROUPS + g) * SSD_STATE:(SSD_GROUPS + g + 1) * SSD_STATE]
            st["bmb"] = bm.astype(BF16)
            st["cb2"] = _bdot_nt(cm, jnp.concatenate([st["bmb"], st["bmb"]], axis=0))
            st["yoff"] = _bdot(cm, ht_s[:, gs])
            st["xw"], st["elast"] = [], []

        def pair(jj):
            j = g * ppg + jj
            r0, r1 = 2 * j, 2 * j + 1
            sl = slice(j * LANES, (j + 1) * LANES)
            rowp = jnp.where(lo_half, lane_bcast(cum, r0), lane_bcast(cum, r1))
            wb = jnp.where(lo_half, lane_bcast(wsm, r0), lane_bcast(wsm, r1))
            colp = pair_cols(cumt, r0, r1)
            dtrow = pair_cols(vmt, r0, r1)
            lmat = jnp.exp2(jnp.where(incl2, rowp - colp, -jnp.inf))
            mp = st["cb2"] * lmat * dtrow
            xsp = xs_s[:, sl]
            e1 = jnp.exp2(rowp)
            y = (_bdot(mp, blockdiag(xsp)) + st["yoff"][:, jj * LANES:(jj + 1) * LANES] * e1
                 + xsp * dexp_ref[:, sl])
            y_s[:, sl] = y * _silu(za_ref[:, sl])
            st["xw"].append((xsp * wb).astype(BF16))
            st["elast"].append(e1[T - 1:T, :])

        def epilogue():
            xw = jnp.concatenate(st["xw"], axis=1)
            elast = jnp.concatenate(st["elast"], axis=1)
            ht_s[:, gs] = ht_s[:, gs] * elast + _bdot_tn(st["bmb"], xw)
            yg = y_s[:, gs]
            ms = jnp.mean(yg * yg, axis=-1, keepdims=True)
            ya_ref[:, gs] = ((yg * lax.rsqrt(ms + EPS)) * nwa_ref[:, gs]).astype(BF16)

        return [prologue] + [functools.partial(pair, jj) for jj in range(ppg)] + [epilogue]

    fillers = (conv_tasks(v_ref, hb, cwb_ref, None, 2 * GDN_WIDTH, v_s, GDN_WIDTH)
               + conv_tasks(xs_ref, ha, cwa_ref, cba_ref, 0, xs_s, SSD_D_INNER)
               + conv_tasks(bc_ref, ha, cwa_ref, cba_ref, SSD_D_INNER, bc_s, SSD_BC))
    for g in range(SSD_GROUPS):
        fillers += ssd_group_tasks(g)
    n_waves = 11
    per_wave = -(-len(fillers) // n_waves)

    def fill():
        for _ in range(min(per_wave, len(fillers))):
            fillers.pop(0)()

    for t in conv_tasks(q_ref, hb, cwb_ref, None, 0, q_s, GDN_WIDTH) + conv_tasks(k_ref, hb, cwb_ref, None, GDN_WIDTH, k_s, GDN_WIDTH):
        t()
    q_scale = GDN_HEAD_DIM ** -0.5
    eye2 = jnp.where(row == scol, 1.0, 0.0).astype(F32)
    blk16 = (row // 16) == (scol // 16)
    blk32 = (row // 32) == (scol // 32)
    zb = jnp.zeros((T, LANES), BF16)
    zb2 = jnp.zeros((T, 2 * LANES), BF16)
    npair = GDN_HEADS // 2

    qn, kn, gcb, betab, eg = [], [], [], [], []
    for h in range(GDN_HEADS):
        sl = slice(h * LANES, (h + 1) * LANES)
        qh = q_s[:, sl]
        kh = k_s[:, sl]
        qn.append(qh * (lax.rsqrt(jnp.sum(qh * qh, axis=-1, keepdims=True) + EPS) * q_scale))
        kn.append(kh * lax.rsqrt(jnp.sum(kh * kh, axis=-1, keepdims=True) + EPS))
        gcb.append(lane_bcast(cum, SSD_HEADS + h))
        betab.append(lane_bcast(vmat, SSD_HEADS + h))
        eg.append(jnp.exp2(gcb[-1]))

    decayp, qkp, ap = [], [], []
    for i in range(npair):
        h0, h1 = 2 * i, 2 * i + 1
        growp = jnp.where(lo_half, gcb[h0], gcb[h1])
        betap = jnp.where(lo_half, betab[h0], betab[h1])
        gcolp = pair_cols(cumt, SSD_HEADS + h0, SSD_HEADS + h1)
        decayp.append(jnp.exp2(jnp.where(incl2, growp - gcolp, -jnp.inf)))
        knb = [kn[h0].astype(BF16), kn[h1].astype(BF16)]
        lhs = jnp.concatenate([jnp.concatenate([qn[h0].astype(BF16), qn[h1].astype(BF16)], axis=1),
                               jnp.concatenate(knb, axis=1)], axis=0)
        rhst = jnp.concatenate([jnp.concatenate([knb[0], zb], axis=1),
                                jnp.concatenate([zb, knb[1]], axis=1)], axis=0)
        qkkk = _bdot_nt(lhs, rhst)
        qkp.append(qkkk[0:T])
        ap.append(jnp.where(strict2, betap * qkkk[T:2 * T] * decayp[i], 0.0))
    fill()

    d = [jnp.where(blk16, a, 0.0) for a in ap]
    pw = [_bdot(di, blockdiag(di)) for di in d]
    x = [eye2 - di for di in d]
    fill()
    for lvl in range(3):
        bdp = [blockdiag(p_) for p_ in pw]
        if lvl < 2:
            r = [_bdot(jnp.concatenate([x[i], pw[i]], axis=0), bdp[i]) for i in range(npair)]
            x = [x[i] + r[i][0:T] for i in range(npair)]
            pw = [r[i][T:2 * T] for i in range(npair)]
        else:
            x = [x[i] + _bdot(x[i], bdp[i]) for i in range(npair)]
        fill()
    for lvl in range(2):
        if lvl == 0:
            off = [jnp.where(blk32 & ~blk16, a, 0.0) for a in ap]
        else:
            off = [jnp.where(blk32, 0.0, a) for a in ap]
        y = [_bdot(x[i], blockdiag(off[i])) for i in range(npair)]
        fill()
        x = [x[i] - _bdot(y[i], blockdiag(x[i])) for i in range(npair)]
        fill()

    uw = []
    for i in range(npair):
        rhs = [jnp.concatenate([(v_s[:, h * LANES:(h + 1) * LANES] * betab[h]).astype(BF16),
                                (kn[h] * (betab[h] * eg[h])).astype(BF16)], axis=1)
               for h in (2 * i, 2 * i + 1)]
        bdr = jnp.concatenate([jnp.concatenate([rhs[0], zb2], axis=1),
                               jnp.concatenate([zb2, rhs[1]], axis=1)], axis=0)
        uw.append(_bdot(x[i], bdr))
    fill()

    wv, qs = [], []
    for h in range(GDN_HEADS):
        i, e = divmod(h, 2)
        u = uw[i][:, e * 2 * LANES:e * 2 * LANES + LANES]
        w = uw[i][:, e * 2 * LANES + LANES:(e + 1) * 2 * LANES]
        wq = _bdot(jnp.concatenate([w, qn[h] * eg[h]], axis=0), s_s[h])
        wv.append((u - wq[0:T]).astype(BF16))
        qs.append(wq[T:2 * T])
    fill()

    oo = []
    for i in range(npair):
        bdw = jnp.concatenate([jnp.concatenate([wv[2 * i], zb], axis=1),
                               jnp.concatenate([zb, wv[2 * i + 1]], axis=1)], axis=0)
        oo.append(_bdot(qkp[i] * decayp[i], bdw))
    fill()
    while fillers:
        fill()
    for h in range(GDN_HEADS):
        i, e = divmod(h, 2)
        sl = slice(h * LANES, (h + 1) * LANES)
        oh = qs[h] + oo[i][:, e * LANES:(e + 1) * LANES]
        last = gcb[h][T - 1:T, :]
        kdec = kn[h] * jnp.exp2(last - gcb[h])
        s_s[h] = s_s[h] * eg[h][T - 1:T, :] + _bdot_tn(kdec, wv[h])
        ms = jnp.mean(oh * oh, axis=-1, keepdims=True)
        on = (oh * lax.rsqrt(ms + EPS)) * nwb_ref[...]
        ob_ref[:, sl] = (on * _silu(zb_ref[:, sl])).astype(BF16)

    @pl.when(c == nc - 1)
    def _():
        for j in range(SSD_PAIRS):
            ssd_o[j] = ht_s[:, j * LANES:(j + 1) * LANES].T
        gdn_o[...] = s_s[...]


def _mixer(p, ps, conv_a, ssm_a, conv_b, ssm_b, prm, n_seq, seq_len, n_pad):
    nc = seq_len // CHUNK
    m = n_seq * seq_len
    T = CHUNK

    def rows(blk):
        return lambda b, c: (b * nc + c, blk)

    def state_map(arr):
        shared = arr.shape[0] == 1
        nd = arr.ndim
        return lambda b, c: ((0 if shared else b),) + (0,) * (nd - 1)

    def full(arr):
        nd = arr.ndim
        return pl.BlockSpec(arr.shape, lambda b, c: (0,) * nd)

    ssm_a4 = ssm_a.reshape(ssm_a.shape[0], SSD_PAIRS, 2 * SSD_HEAD_DIM, SSD_STATE)
    in_specs = [
        pl.BlockSpec((T, WIDE), rows(BLK_ZA)),
        pl.BlockSpec((T, WIDE), rows(BLK_XS)),
        pl.BlockSpec((T, WIDE), rows(BLK_Q)),
        pl.BlockSpec((T, WIDE), rows(BLK_K)),
        pl.BlockSpec((T, WIDE), rows(BLK_V)),
        pl.BlockSpec((T, WIDE), rows(BLK_ZB)),
        pl.BlockSpec((T, SSD_BC), rows(BLK_BC)),
        pl.BlockSpec((T, SMALL_COLS), rows(0)),
        pl.BlockSpec((None, CONV_K - 1, SSD_CONV_CH), state_map(conv_a)),
        pl.BlockSpec((None, SSD_PAIRS, 2 * SSD_HEAD_DIM, SSD_STATE), state_map(ssm_a4)),
        pl.BlockSpec((None, CONV_K - 1, GDN_CONV_CH), state_map(conv_b)),
        pl.BlockSpec((None, GDN_HEADS, GDN_HEAD_DIM, GDN_HEAD_DIM), state_map(ssm_b)),
    ] + [full(a) for a in prm]
    out_specs = [
        pl.BlockSpec((T, SSD_D_INNER), lambda b, c: (b * nc + c, 0)),
        pl.BlockSpec((T, GDN_WIDTH), lambda b, c: (b * nc + c, 0)),
        pl.BlockSpec((None, CONV_K - 1, SSD_CONV_CH), lambda b, c: (b, 0, 0)),
        pl.BlockSpec((None, SSD_PAIRS, 2 * SSD_HEAD_DIM, SSD_STATE), lambda b, c: (b, 0, 0, 0)),
        pl.BlockSpec((None, CONV_K - 1, GDN_CONV_CH), lambda b, c: (b, 0, 0)),
        pl.BlockSpec((None, GDN_HEADS, GDN_HEAD_DIM, GDN_HEAD_DIM), lambda b, c: (b, 0, 0, 0)),
    ]
    out_shape = [
        jax.ShapeDtypeStruct((m, SSD_D_INNER), BF16),
        jax.ShapeDtypeStruct((m, GDN_WIDTH), BF16),
        jax.ShapeDtypeStruct((n_seq, CONV_K - 1, SSD_CONV_CH), F32),
        jax.ShapeDtypeStruct((n_seq, SSD_PAIRS, 2 * SSD_HEAD_DIM, SSD_STATE), F32),
        jax.ShapeDtypeStruct((n_seq, CONV_K - 1, GDN_CONV_CH), F32),
        jax.ShapeDtypeStruct((n_seq, GDN_HEADS, GDN_HEAD_DIM, GDN_HEAD_DIM), F32),
    ]
    scratch = [
        pltpu.VMEM((8, SSD_CONV_CH), F32),
        pltpu.VMEM((8, GDN_CONV_CH), F32),
        pltpu.VMEM((T, SSD_D_INNER), F32),
        pltpu.VMEM((T, SSD_BC), F32),
        pltpu.VMEM((T, GDN_WIDTH), F32),
        pltpu.VMEM((T, GDN_WIDTH), F32),
        pltpu.VMEM((T, GDN_WIDTH), F32),
        pltpu.VMEM((SSD_STATE, SSD_D_INNER), F32),
        pltpu.VMEM((GDN_HEADS, GDN_HEAD_DIM, GDN_HEAD_DIM), F32),
        pltpu.VMEM((T, SSD_D_INNER), F32),
    ]
    outs = pl.pallas_call(
        functools.partial(_mixer_kernel, n_pad=n_pad),
        grid=(n_seq, nc),
        in_specs=in_specs,
        out_specs=out_specs,
        out_shape=out_shape,
        scratch_shapes=scratch,
        compiler_params=pltpu.CompilerParams(
            dimension_semantics=("arbitrary", "arbitrary"), vmem_limit_bytes=VMEM_LIMIT),
        name="mixer",
    )(p, p, p, p, p, p, p, ps, conv_a, ssm_a4, conv_b, ssm_b, *prm)
    ya, ob, cva, ssd, cvb, gdn = outs
    return ya, ob, cva, ssd.reshape(n_seq, SSD_HEADS, SSD_HEAD_DIM, SSD_STATE), cvb, gdn


def _outproj_kernel(x_ref, ya_ref, ob_ref, ga_ref, gb_ref, wa_ref, wb_ref, nw_ref, h_ref, n_ref):
    ma = jnp.dot(ya_ref[...], wa_ref[...], preferred_element_type=F32)
    mb = jnp.dot(ob_ref[...], wb_ref[...], preferred_element_type=F32)
    h = x_ref[...] + _sigmoid(ga_ref[...]) * ma + _sigmoid(gb_ref[...]) * mb
    h_ref[...] = h
    ms = jnp.mean(h * h, axis=-1, keepdims=True)
    n_ref[...] = ((h * lax.rsqrt(ms + EPS)) * nw_ref[...]).astype(BF16)


def _outproj(x2d, ya, ob, p, w_out_a, w_out_b, norm_w, tm):
    m = x2d.shape[0]
    const = lambda i: (0, 0)
    return pl.pallas_call(
        _outproj_kernel,
        grid=(m // tm,),
        in_specs=[
            pl.BlockSpec((tm, D_MODEL), lambda i: (i, 0)),
            pl.BlockSpec((tm, SSD_D_INNER), lambda i: (i, 0)),
            pl.BlockSpec((tm, GDN_WIDTH), lambda i: (i, 0)),
            pl.BlockSpec((tm, WIDE), lambda i: (i, BLK_GA)),
            pl.BlockSpec((tm, WIDE), lambda i: (i, BLK_GB)),
            pl.BlockSpec((SSD_D_INNER, D_MODEL), const, pipeline_mode=pl.Buffered(1)),
            pl.BlockSpec((GDN_WIDTH, D_MODEL), const, pipeline_mode=pl.Buffered(1)),
            pl.BlockSpec((1, D_MODEL), const),
        ],
        out_specs=[
            pl.BlockSpec((tm, D_MODEL), lambda i: (i, 0)),
            pl.BlockSpec((tm, D_MODEL), lambda i: (i, 0)),
        ],
        out_shape=[
            jax.ShapeDtypeStruct((m, D_MODEL), F32),
            jax.ShapeDtypeStruct((m, D_MODEL), BF16),
        ],
        compiler_params=pltpu.CompilerParams(
            dimension_semantics=("arbitrary",), vmem_limit_bytes=VMEM_LIMIT),
        name="outproj",
    )(x2d, ya, ob, p, p, w_out_a, w_out_b, norm_w)


def _mlp_kernel(n_ref, h_ref, wu_ref, wd_ref, nf_ref, y_ref, hid_s, acc_s, *, n_up, tf, tn):
    j = pl.program_id(1)

    @pl.when(j < n_up)
    def _():
        hid = jnp.dot(n_ref[...], wu_ref[...], preferred_element_type=F32)
        off = pl.multiple_of(j * tf, tf)
        hid_s[:, pl.ds(off, tf)] = jnp.square(jnp.maximum(hid, 0.0)).astype(BF16)

    @pl.when(j >= n_up)
    def _():
        off = pl.multiple_of((j - n_up) * tn, tn)
        acc_s[:, pl.ds(off, tn)] = h_ref[:, pl.ds(off, tn)] + jnp.dot(hid_s[...], wd_ref[...],
                                                                      preferred_element_type=F32)

    @pl.when(j == pl.num_programs(1) - 1)
    def _():
        h = acc_s[...]
        ms = jnp.mean(h * h, axis=-1, keepdims=True)
        y_ref[...] = (h * lax.rsqrt(ms + EPS)) * nf_ref[...]


def _mlp(n, h, w_up, w_down, norm_f, tm, tf=1024, tn=256):
    m = n.shape[0]
    n_up, n_down = D_FF // tf, D_MODEL // tn
    return pl.pallas_call(
        functools.partial(_mlp_kernel, n_up=n_up, tf=tf, tn=tn),
        grid=(m // tm, n_up + n_down),
        in_specs=[
            pl.BlockSpec((tm, D_MODEL), lambda i, j: (i, 0)),
            pl.BlockSpec((tm, D_MODEL), lambda i, j: (i, 0)),
            pl.BlockSpec((D_MODEL, tf), lambda i, j: (0, jnp.minimum(j, n_up - 1))),
            pl.BlockSpec((D_FF, tn), lambda i, j: (0, jnp.maximum(j - n_up, 0))),
            pl.BlockSpec((1, D_MODEL), lambda i, j: (0, 0)),
        ],
        out_specs=pl.BlockSpec((tm, D_MODEL), lambda i, j: (i, 0)),
        out_shape=jax.ShapeDtypeStruct((m, D_MODEL), F32),
        scratch_shapes=[pltpu.VMEM((tm, D_FF), BF16), pltpu.VMEM((tm, D_MODEL), F32)],
        compiler_params=pltpu.CompilerParams(
            dimension_semantics=("arbitrary", "arbitrary"), vmem_limit_bytes=VMEM_LIMIT),
        name="mlp",
    )(n, h, w_up, w_down, norm_f)


def _row_tile(m, cap):
    t = min(m, cap)
    while m % t:
        t //= 2
    return t


def kernel(x_prompt, x_sample, state_ssd_conv, state_ssd, state_gdn_conv, state_gdn, meta_tokens, norm_mix_w, w_in, ssd_conv_w, ssd_conv_b, ssd_dt_bias, ssd_a_log, ssd_d, ssd_norm_w, gdn_conv_w, gdn_dt_bias, gdn_a_log, gdn_norm_w, w_out, norm_mlp_w, w_up, w_down, norm_f_w):
    bp, sp_len, _ = x_prompt.shape
    bs, ss_len, _ = x_sample.shape
    assert w_in.shape[0] == 1, "single layer"
    assert sp_len % CHUNK == 0 and ss_len % CHUNK == 0 and N_META <= CHUNK

    wi = w_in[0]
    z_a, xbc_a, dt_a, qkv_b, z_b, beta_b, a_b, gate_a, gate_b = jnp.split(
        wi, [2048, 5120, 5152, 11296, 13344, 13360, 13376, 15424], axis=1)
    xs_w, bc_w = xbc_a[:, :SSD_D_INNER], xbc_a[:, SSD_D_INNER:]
    q_w, k_w, v_w = jnp.split(qkv_b, 3, axis=1)
    w_main = jnp.concatenate([z_a, xs_w, q_w, k_w, v_w, z_b, gate_a, gate_b, bc_w], axis=1).astype(BF16)
    zc = lambda n: jnp.zeros((D_MODEL, n), wi.dtype)
    w_small = jnp.concatenate([dt_a, a_b, zc(LANES - 48), zc(32), beta_b, zc(LANES - 48)], axis=1).astype(BF16)
    w_out_a = w_out[0, :SSD_D_INNER].astype(BF16)
    w_out_b = w_out[0, SSD_D_INNER:].astype(BF16)
    w_up_b = w_up[0].astype(BF16)
    w_down_b = w_down[0].astype(BF16)

    zrow = lambda n: jnp.zeros((1, n), F32)
    sbias = jnp.concatenate([ssd_dt_bias[0][None], gdn_dt_bias[0][None], zrow(LANES - 48)], axis=1)
    alog = jnp.concatenate([ssd_a_log[0][None], gdn_a_log[0][None], zrow(LANES - 48)], axis=1)
    dexp = jnp.repeat(ssd_d[0], SSD_HEAD_DIM)[None]
    prm = (ssd_conv_w[0], ssd_conv_b[0][None], gdn_conv_w[0], sbias, alog, dexp,
           ssd_norm_w[0][None], gdn_norm_w[0][None])
    nmix = norm_mix_w[0][None]
    nmlp = norm_mlp_w[0][None]
    nf = norm_f_w[None]

    def front(x2d, n_seq, seq_len, n_pad, conv_a, ssm_a, conv_b, ssm_b):
        u = _norm(x2d, nmix, _row_tile(x2d.shape[0], 1024))
        p, ps = _inproj(u, w_main, w_small, _row_tile(x2d.shape[0], 2048))
        return p, _mixer(p, ps, conv_a, ssm_a, conv_b, ssm_b, prm, n_seq, seq_len, n_pad)

    def back(x2d, p, ya, ob):
        h, n = _outproj(x2d, ya, ob, p, w_out_a, w_out_b, nmlp, _row_tile(x2d.shape[0], 256))
        return _mlp(n, h, w_up_b, w_down_b, nf, _row_tile(x2d.shape[0], 512))

    dtp = x_prompt.dtype
    n_pad = CHUNK - N_META
    x_meta = jnp.concatenate([jnp.zeros((n_pad, D_MODEL), dtp), meta_tokens.astype(dtp)], axis=0)
    _, (_, _, cva_m, ssd_m, cvb_m, gdn_m) = front(
        x_meta, 1, CHUNK, n_pad,
        jnp.zeros((1, CONV_K - 1, SSD_CONV_CH), dtp), jnp.zeros((1, SSD_HEADS, SSD_HEAD_DIM, SSD_STATE), dtp),
        jnp.zeros((1, CONV_K - 1, GDN_CONV_CH), dtp), jnp.zeros((1, GDN_HEADS, GDN_HEAD_DIM, GDN_HEAD_DIM), dtp))

    xp2 = x_prompt.reshape(bp * sp_len, D_MODEL)
    p_p, (ya_p, ob_p, cva_p, ssd_p, cvb_p, gdn_p) = front(xp2, bp, sp_len, 0, cva_m, ssd_m, cvb_m, gdn_m)
    y_prompt = back(xp2, p_p, ya_p, ob_p).reshape(bp, sp_len, D_MODEL)

    xs2 = x_sample.reshape(bs * ss_len, D_MODEL)
    p_s, (ya_s, ob_s, cva_s, ssd_s, cvb_s, gdn_s) = front(
        xs2, bs, ss_len, 0, state_ssd_conv[0], state_ssd[0], state_gdn_conv[0], state_gdn[0])
    y_sample = back(xs2, p_s, ya_s, ob_s).reshape(bs, ss_len, D_MODEL)

    return (y_prompt, y_sample, cva_p[None], ssd_p[None], cvb_p[None], gdn_p[None],
            cva_s[None], ssd_s[None], cvb_s[None], gdn_s[None])
```

```python
import functools

import jax
import jax.numpy as jnp
from jax import lax
from jax.experimental import pallas as pl
from jax.experimental.pallas import tpu as pltpu

F32 = jnp.float32
BF16 = jnp.bfloat16
HIGHEST = lax.Precision.HIGHEST

D_MODEL = 2048
CHUNK = 64
N_META = 16
CONV_K = 4
EPS = 1e-6
LOG2E = 1.4426950408889634

SSD_HEADS = 32
SSD_HEAD_DIM = 64
SSD_GROUPS = 4
SSD_STATE = 128
SSD_D_INNER = SSD_HEADS * SSD_HEAD_DIM
SSD_BC = 2 * SSD_GROUPS * SSD_STATE
SSD_CONV_CH = SSD_D_INNER + SSD_BC
SSD_PAIRS = SSD_HEADS // 2

GDN_HEADS = 16
GDN_HEAD_DIM = 128
GDN_WIDTH = GDN_HEADS * GDN_HEAD_DIM
GDN_CONV_CH = 3 * GDN_WIDTH
D_FF = 4 * D_MODEL

LANES = 128
WIDE = 2048
BLK_ZA, BLK_XS, BLK_Q, BLK_K, BLK_V, BLK_ZB, BLK_GA, BLK_GB = range(8)
P_COLS = 8 * WIDE + SSD_BC
BLK_BC = (8 * WIDE) // SSD_BC
SMALL_COLS = 2 * LANES

VMEM_LIMIT = 56 * 1024 * 1024


def _silu(x):
    h = 0.5 * x
    return h * jnp.tanh(h) + h


def _sigmoid(x):
    return 1.0 / (1.0 + jnp.exp(-x))


def _softplus(x):
    return jnp.maximum(x, 0.0) + jnp.log1p(jnp.exp(-jnp.abs(x)))


def _bdot(a, b):
    return jnp.dot(a.astype(BF16), b.astype(BF16), preferred_element_type=F32)


def _bdot_nt(a, b):
    return lax.dot_general(a.astype(BF16), b.astype(BF16), (((1,), (1,)), ((), ())), preferred_element_type=F32)


def _bdot_tn(a, b):
    return lax.dot_general(a.astype(BF16), b.astype(BF16), (((0,), (0,)), ((), ())), preferred_element_type=F32)


def _inproj_kernel(x_ref, nw_ref, w_ref, ws_ref, p_ref, ps_ref, u_s):
    j = pl.program_id(1)

    @pl.when(j == 0)
    def _():
        x = x_ref[...]
        ms = jnp.mean(x * x, axis=-1, keepdims=True)
        u = (x * lax.rsqrt(ms + EPS)) * nw_ref[...]
        u_s[...] = u.astype(BF16)
        ps_ref[...] = jnp.dot(u_s[...], ws_ref[...], preferred_element_type=F32)

    p_ref[...] = jnp.dot(u_s[...], w_ref[...], preferred_element_type=F32)


def _inproj(x2d, norm_w, w_main, w_small, tm, tn=1024):
    m = x2d.shape[0]
    grid = (m // tm, P_COLS // tn)
    return pl.pallas_call(
        _inproj_kernel,
        grid=grid,
        in_specs=[
            pl.BlockSpec((tm, D_MODEL), lambda i, j: (i, 0)),
            pl.BlockSpec((1, D_MODEL), lambda i, j: (0, 0)),
            pl.BlockSpec((D_MODEL, tn), lambda i, j: (0, j)),
            pl.BlockSpec((D_MODEL, SMALL_COLS), lambda i, j: (0, 0)),
        ],
        out_specs=[
            pl.BlockSpec((tm, tn), lambda i, j: (i, j)),
            pl.BlockSpec((tm, SMALL_COLS), lambda i, j: (i, 0)),
        ],
        out_shape=[
            jax.ShapeDtypeStruct((m, P_COLS), F32),
            jax.ShapeDtypeStruct((m, SMALL_COLS), F32),
        ],
        scratch_shapes=[pltpu.VMEM((tm, D_MODEL), BF16)],
        compiler_params=pltpu.CompilerParams(
            dimension_semantics=("arbitrary", "arbitrary"), vmem_limit_bytes=VMEM_LIMIT),
        name="inproj",
    )(x2d, norm_w, w_main, w_small)


def _mixer_kernel(za_ref, xs_ref, q_ref, k_ref, v_ref, zb_ref, bc_ref, sm_ref,
                  cva_ref, ssd_ref, cvb_ref, gdn_ref,
                  cwa_ref, cba_ref, cwb_ref, sbias_ref, alog_ref, dexp_ref, nwa_ref, nwb_ref,
                  ya_ref, ob_ref, cva_o, ssd_o, cvb_o, gdn_o,
                  ha, hb, xs_s, bc_s, q_s, k_s, v_s, ht_s, s_s, y_s, rhs_s, qe_s, kdec_s, qkd_s, *, n_pad, n_sub):
    c = pl.program_id(1)
    nc = pl.num_programs(1)
    T = CHUNK
    HIST = CONV_K - 1
    rows_blk = n_sub * T

    @pl.when(c == 0)
    def _():
        ha[...] = jnp.concatenate([jnp.zeros((8 - HIST, SSD_CONV_CH), F32), cva_ref[...]], axis=0)
        hb[...] = jnp.concatenate([jnp.zeros((8 - HIST, GDN_CONV_CH), F32), cvb_ref[...]], axis=0)
        for j in range(SSD_PAIRS):
            ht_s[:, j * LANES:(j + 1) * LANES] = ssd_ref[j].T
        s_s[...] = gdn_ref[...]

    @pl.when(c == nc - 1)
    def _():
        cva_o[:, 0:SSD_D_INNER] = xs_ref[rows_blk - HIST:rows_blk, :]
        cva_o[:, SSD_D_INNER:SSD_CONV_CH] = bc_ref[rows_blk - HIST:rows_blk, :]
        cvb_o[:, 0:GDN_WIDTH] = q_ref[rows_blk - HIST:rows_blk, :]
        cvb_o[:, GDN_WIDTH:2 * GDN_WIDTH] = k_ref[rows_blk - HIST:rows_blk, :]
        cvb_o[:, 2 * GDN_WIDTH:3 * GDN_WIDTH] = v_ref[rows_blk - HIST:rows_blk, :]

    lane = lax.broadcasted_iota(jnp.int32, (T, LANES), 1)
    row = lax.broadcasted_iota(jnp.int32, (T, LANES), 0)
    tr = lax.broadcasted_iota(jnp.int32, (T, T), 0)
    tc = lax.broadcasted_iota(jnp.int32, (T, T), 1)
    tril = jnp.where(tr >= tc, 1.0, 0.0).astype(F32)
    zpad = jnp.zeros((T, LANES), F32)
    lo_half = lane < T
    scol = lane & (T - 1)
    incl2 = row >= scol
    strict2 = row > scol
    mask_lo = jnp.where(lo_half, 1.0, 0.0).astype(BF16)
    mask_hi = jnp.where(lo_half, 0.0, 1.0).astype(BF16)
    eye2 = jnp.where(row == scol, 1.0, 0.0).astype(F32)
    blk16 = (row // 16) == (scol // 16)
    blk32 = (row // 32) == (scol // 32)
    zb = jnp.zeros((T, LANES), BF16)
    zb2 = jnp.zeros((T, 2 * LANES), BF16)
    q_scale = GDN_HEAD_DIM ** -0.5
    npair = GDN_HEADS // 2
    gw = SSD_D_INNER // SSD_GROUPS
    ppg = SSD_PAIRS // SSD_GROUPS

    def lane_bcast(m, col):
        return jnp.broadcast_to(m[:, col:col + 1], (T, LANES))

    def pair_cols(mt, r0, r1):
        return jnp.concatenate([mt[r0:r0 + 1, 0:T], mt[r1:r1 + 1, 0:T]], axis=1)

    def blockdiag(yp):
        yb = yp.astype(BF16)
        return jnp.concatenate([yb * mask_lo, yb * mask_hi], axis=0)

    fillers = []

    def chunk_program(u):
        rs = slice(u * T, (u + 1) * T)
        xs_u, bc_u, q_u, k_u, v_u, y_u = xs_s.at[u], bc_s.at[u], q_s.at[u], k_s.at[u], v_s.at[u], y_s.at[u]
        rhs_u, qe_u, kdec_u, qkd_u = rhs_s.at[u], qe_s.at[u], kdec_s.at[u], qkd_s.at[u]

        def conv_slab(in_ref, i_lo, hist, w_ref, b_ref, c_lo, out_ref, o_lo, wd=512):
            ci = slice(i_lo, i_lo + wd)
            cc = slice(c_lo, c_lo + wd)
            x = in_ref[rs, ci]
            full = jnp.concatenate([hist[0:8, cc], x], axis=0)
            w = [w_ref[j:j + 1, cc] for j in range(CONV_K)]
            f1 = pltpu.roll(full, 1, axis=0)
            pair_b = full * w[1] + f1 * w[0]
            y = x * w[3] + f1[8:8 + T] * w[2] + pltpu.roll(pair_b, 2, axis=0)[8:8 + T]
            if b_ref is not None:
                y = y + b_ref[:, cc]
            out_ref[:, o_lo:o_lo + wd] = _silu(y)
            hist[0:8, cc] = x[T - 8:T]

        def conv_tasks(in_ref, hist, w_ref, b_ref, c_base, out_ref, width, wd=512):
            return [functools.partial(conv_slab, in_ref, s * wd, hist, w_ref, b_ref, c_base + s * wd, out_ref, s * wd, wd)
                    for s in range(width // wd)]

        sp = _softplus(sm_ref[rs, 0:LANES] + sbias_ref[...])
        sg = _sigmoid(sm_ref[rs, LANES:2 * LANES])
        neg_a = -jnp.exp(alog_ref[...])
        used = lane < SSD_HEADS + GDN_HEADS
        gmat = jnp.where(used, sp * neg_a, 0.0)
        vmat = jnp.where(lane < SSD_HEADS, sp, jnp.where(used, sg, 0.0))
        if n_pad:
            live = row >= n_pad
            gmat = jnp.where(live, gmat, 0.0)
            vmat = jnp.where(live, vmat, 0.0)
        cum = jnp.dot(tril, gmat, precision=HIGHEST, preferred_element_type=F32) * LOG2E
        cumt = jnp.concatenate([cum, zpad], axis=0).T
        vmt = jnp.concatenate([vmat, zpad], axis=0).T
        last_all = cum[T - 1:T, :]
        wsm = jnp.exp2(last_all - cum) * vmat

        def ssd_group_tasks(g):
            st = {}
            gs = slice(g * gw, (g + 1) * gw)

            def prologue():
                bm = bc_u[:, g * SSD_STATE:(g + 1) * SSD_STATE]
                cm = bc_u[:, (SSD_GROUPS + g) * SSD_STATE:(SSD_GROUPS + g + 1) * SSD_STATE]
                st["bmb"] = bm.astype(BF16)
                st["cb2"] = _bdot_nt(cm, jnp.concatenate([st["bmb"], st["bmb"]], axis=0))
                st["yoff"] = _bdot(cm, ht_s[:, gs])
                st["xw"], st["elast"] = [], []

            def pair(jj):
                j = g * ppg + jj
                r0, r1 = 2 * j, 2 * j + 1
                sl = slice(j * LANES, (j + 1) * LANES)
                rowp = jnp.where(lo_half, lane_bcast(cum, r0), lane_bcast(cum, r1))
                wb = jnp.where(lo_half, lane_bcast(wsm, r0), lane_bcast(wsm, r1))
                colp = pair_cols(cumt, r0, r1)
                dtrow = pair_cols(vmt, r0, r1)
                lmat = jnp.exp2(jnp.where(incl2, rowp - colp, -jnp.inf))
                mp = st["cb2"] * lmat * dtrow
                xsp = xs_u[:, sl]
                e1 = jnp.exp2(rowp)
                y = (_bdot(mp, blockdiag(xsp)) + st["yoff"][:, jj * LANES:(jj + 1) * LANES] * e1
                     + xsp * dexp_ref[:, sl])
                y_u[:, sl] = y * _silu(za_ref[rs, sl])
                st["xw"].append((xsp * wb).astype(BF16))
                st["elast"].append(e1[T - 1:T, :])

            def epilogue():
                xw = jnp.concatenate(st["xw"], axis=1)
                elast = jnp.concatenate(st["elast"], axis=1)
                ht_s[:, gs] = ht_s[:, gs] * elast + _bdot_tn(st["bmb"], xw)
                yg = y_u[:, gs]
                ms = jnp.mean(yg * yg, axis=-1, keepdims=True)
                ya_ref[rs, gs] = ((yg * lax.rsqrt(ms + EPS)) * nwa_ref[:, gs]).astype(BF16)

            return [prologue] + [functools.partial(pair, jj) for jj in range(ppg)] + [epilogue]

        fillers.extend(conv_tasks(xs_ref, ha, cwa_ref, cba_ref, 0, xs_u, SSD_D_INNER)
                       + conv_tasks(bc_ref, ha, cwa_ref, cba_ref, SSD_D_INNER, bc_u, SSD_BC))
        for g in range(SSD_GROUPS):
            fillers.extend(ssd_group_tasks(g))

        for t in (conv_tasks(q_ref, hb, cwb_ref, None, 0, q_u, GDN_WIDTH)
                  + conv_tasks(k_ref, hb, cwb_ref, None, GDN_WIDTH, k_u, GDN_WIDTH)
                  + conv_tasks(v_ref, hb, cwb_ref, None, 2 * GDN_WIDTH, v_u, GDN_WIDTH)):
            t()

        qkp, decayp, ap, eg_last = [], [], [], []
        for i in range(npair):
            qn, kn, gcb, betab = [], [], [], []
            for h in (2 * i, 2 * i + 1):
                sl = slice(h * LANES, (h + 1) * LANES)
                qh = q_u[:, sl]
                kh = k_u[:, sl]
                qn.append(qh * (lax.rsqrt(jnp.sum(qh * qh, axis=-1, keepdims=True) + EPS) * q_scale))
                kn.append(kh * lax.rsqrt(jnp.sum(kh * kh, axis=-1, keepdims=True) + EPS))
                gcb.append(lane_bcast(cum, SSD_HEADS + h))
                betab.append(lane_bcast(vmat, SSD_HEADS + h))
                eg = jnp.exp2(gcb[-1])
                eg_last.append(eg[T - 1:T, :])
                rhs_u[:, 2 * h * LANES:(2 * h + 1) * LANES] = (v_u[:, sl] * betab[-1]).astype(BF16)
                rhs_u[:, (2 * h + 1) * LANES:(2 * h + 2) * LANES] = (kn[-1] * (betab[-1] * eg)).astype(BF16)
                qe_u[:, sl] = (qn[-1] * eg).astype(BF16)
                kdec_u[:, sl] = (kn[-1] * jnp.exp2(gcb[-1][T - 1:T, :] - gcb[-1])).astype(BF16)
            growp = jnp.where(lo_half, gcb[0], gcb[1])
            betap = jnp.where(lo_half, betab[0], betab[1])
            gcolp = pair_cols(cumt, SSD_HEADS + 2 * i, SSD_HEADS + 2 * i + 1)
            decayp.append(jnp.exp2(jnp.where(incl2, growp - gcolp, -jnp.inf)))
            knb = [kn[0].astype(BF16), kn[1].astype(BF16)]
            lhs = jnp.concatenate([jnp.concatenate([qn[0].astype(BF16), qn[1].astype(BF16)], axis=1),
                                   jnp.concatenate(knb, axis=1)], axis=0)
            rhst = jnp.concatenate([jnp.concatenate([knb[0], zb], axis=1),
                                    jnp.concatenate([zb, knb[1]], axis=1)], axis=0)
            qkkk = _bdot_nt(lhs, rhst)
            qkp.append(qkkk[0:T])
            ap.append(jnp.where(strict2, betap * qkkk[T:2 * T] * decayp[i], 0.0))
        yield

        for i in range(npair):
            qkd_u[:, i * LANES:(i + 1) * LANES] = (qkp[i] * decayp[i]).astype(BF16)
        d = [jnp.where(blk16, a, 0.0) for a in ap]
        pw = [_bdot(di, blockdiag(di)) for di in d]
        x = [eye2 - di for di in d]
        yield
        for lvl in range(3):
            bdp = [blockdiag(p_) for p_ in pw]
            if lvl < 2:
                r = [_bdot(jnp.concatenate([x[i], pw[i]], axis=0), bdp[i]) for i in range(npair)]
                x = [x[i] + r[i][0:T] for i in range(npair)]
                pw = [r[i][T:2 * T] for i in range(npair)]
            else:
                x = [x[i] + _bdot(x[i], bdp[i]) for i in range(npair)]
            yield
        for lvl in range(2):
            if lvl == 0:
                off = [jnp.where(blk32 & ~blk16, a, 0.0) for a in ap]
            else:
                off = [jnp.where(blk32, 0.0, a) for a in ap]
            y = [_bdot(x[i], blockdiag(off[i])) for i in range(npair)]
            yield
            x = [x[i] - _bdot(y[i], blockdiag(x[i])) for i in range(npair)]
            yield

        uw = []
        for i in range(npair):
            r0 = rhs_u[:, 4 * i * LANES:(4 * i + 2) * LANES]
            r1 = rhs_u[:, (4 * i + 2) * LANES:(4 * i + 4) * LANES]
            bdr = jnp.concatenate([jnp.concatenate([r0, zb2], axis=1),
                                   jnp.concatenate([zb2, r1], axis=1)], axis=0)
            uw.append(_bdot(x[i], bdr))
        yield

        wv, qs = [], []
        for h in range(GDN_HEADS):
            i, e = divmod(h, 2)
            u_ = uw[i][:, e * 2 * LANES:e * 2 * LANES + LANES]
            w = uw[i][:, e * 2 * LANES + LANES:(e + 1) * 2 * LANES]
            wq = _bdot(jnp.concatenate([w.astype(BF16), qe_u[:, h * LANES:(h + 1) * LANES]], axis=0), s_s[h])
            wv.append((u_ - wq[0:T]).astype(BF16))
            qs.append(wq[T:2 * T])
        yield

        oo = []
        for i in range(npair):
            bdw = jnp.concatenate([jnp.concatenate([wv[2 * i], zb], axis=1),
                                   jnp.concatenate([zb, wv[2 * i + 1]], axis=1)], axis=0)
            oo.append(_bdot(qkd_u[:, i * LANES:(i + 1) * LANES], bdw))
        yield
        for h in range(GDN_HEADS):
            i, e = divmod(h, 2)
            sl = slice(h * LANES, (h + 1) * LANES)
            oh = qs[h] + oo[i][:, e * LANES:(e + 1) * LANES]
            s_s[h] = s_s[h] * eg_last[h] + _bdot_tn(kdec_u[:, sl], wv[h])
            ms = jnp.mean(oh * oh, axis=-1, keepdims=True)
            on = (oh * lax.rsqrt(ms + EPS)) * nwb_ref[...]
            ob_ref[rs, sl] = (on * _silu(zb_ref[rs, sl])).astype(BF16)

    LAG = 6
    progs = [chunk_program(u) for u in range(n_sub)]
    started = 0
    active = []
    wave = 0
    n_waves = 13 + LAG * (n_sub - 1)
    while started < n_sub or active:
        if started < n_sub and wave >= LAG * started:
            active.append(progs[started])
            started += 1
        for p_ in list(active):
            try:
                next(p_)
            except StopIteration:
                active.remove(p_)
        per_wave = -(-len(fillers) // max(n_waves - wave, 1))
        for _ in range(min(per_wave, len(fillers))):
            fillers.pop(0)()
        wave += 1
    while fillers:
        fillers.pop(0)()

    @pl.when(c == nc - 1)
    def _():
        for j in range(SSD_PAIRS):
            ssd_o[j] = ht_s[:, j * LANES:(j + 1) * LANES].T
        gdn_o[...] = s_s[...]


def _mixer(p, ps, conv_a, ssm_a, conv_b, ssm_b, prm, n_seq, seq_len, n_pad):
    n_sub = 2 if seq_len % (2 * CHUNK) == 0 else 1
    nc = seq_len // (n_sub * CHUNK)
    m = n_seq * seq_len
    T = n_sub * CHUNK

    def rows(blk):
        return lambda b, c: (b * nc + c, blk)

    def state_map(arr):
        shared = arr.shape[0] == 1
        nd = arr.ndim
        return lambda b, c: ((0 if shared else b),) + (0,) * (nd - 1)

    def full(arr):
        nd = arr.ndim
        return pl.BlockSpec(arr.shape, lambda b, c: (0,) * nd)

    ssm_a4 = ssm_a.reshape(ssm_a.shape[0], SSD_PAIRS, 2 * SSD_HEAD_DIM, SSD_STATE)
    in_specs = [
        pl.BlockSpec((T, WIDE), rows(BLK_ZA)),
        pl.BlockSpec((T, WIDE), rows(BLK_XS)),
        pl.BlockSpec((T, WIDE), rows(BLK_Q)),
        pl.BlockSpec((T, WIDE), rows(BLK_K)),
        pl.BlockSpec((T, WIDE), rows(BLK_V)),
        pl.BlockSpec((T, WIDE), rows(BLK_ZB)),
        pl.BlockSpec((T, SSD_BC), rows(BLK_BC)),
        pl.BlockSpec((T, SMALL_COLS), rows(0)),
        pl.BlockSpec((None, CONV_K - 1, SSD_CONV_CH), state_map(conv_a)),
        pl.BlockSpec((None, SSD_PAIRS, 2 * SSD_HEAD_DIM, SSD_STATE), state_map(ssm_a4)),
        pl.BlockSpec((None, CONV_K - 1, GDN_CONV_CH), state_map(conv_b)),
        pl.BlockSpec((None, GDN_HEADS, GDN_HEAD_DIM, GDN_HEAD_DIM), state_map(ssm_b)),
    ] + [full(a) for a in prm]
    out_specs = [
        pl.BlockSpec((T, SSD_D_INNER), lambda b, c: (b * nc + c, 0)),
        pl.BlockSpec((T, GDN_WIDTH), lambda b, c: (b * nc + c, 0)),
        pl.BlockSpec((None, CONV_K - 1, SSD_CONV_CH), lambda b, c: (b, 0, 0)),
        pl.BlockSpec((None, SSD_PAIRS, 2 * SSD_HEAD_DIM, SSD_STATE), lambda b, c: (b, 0, 0, 0)),
        pl.BlockSpec((None, CONV_K - 1, GDN_CONV_CH), lambda b, c: (b, 0, 0)),
        pl.BlockSpec((None, GDN_HEADS, GDN_HEAD_DIM, GDN_HEAD_DIM), lambda b, c: (b, 0, 0, 0)),
    ]
    out_shape = [
        jax.ShapeDtypeStruct((m, SSD_D_INNER), BF16),
        jax.ShapeDtypeStruct((m, GDN_WIDTH), BF16),
        jax.ShapeDtypeStruct((n_seq, CONV_K - 1, SSD_CONV_CH), F32),
        jax.ShapeDtypeStruct((n_seq, SSD_PAIRS, 2 * SSD_HEAD_DIM, SSD_STATE), F32),
        jax.ShapeDtypeStruct((n_seq, CONV_K - 1, GDN_CONV_CH), F32),
        jax.ShapeDtypeStruct((n_seq, GDN_HEADS, GDN_HEAD_DIM, GDN_HEAD_DIM), F32),
    ]
    scratch = [
        pltpu.VMEM((8, SSD_CONV_CH), F32),
        pltpu.VMEM((8, GDN_CONV_CH), F32),
        pltpu.VMEM((n_sub, CHUNK, SSD_D_INNER), F32),
        pltpu.VMEM((n_sub, CHUNK, SSD_BC), F32),
        pltpu.VMEM((n_sub, CHUNK, GDN_WIDTH), F32),
        pltpu.VMEM((n_sub, CHUNK, GDN_WIDTH), F32),
        pltpu.VMEM((n_sub, CHUNK, GDN_WIDTH), F32),
        pltpu.VMEM((SSD_STATE, SSD_D_INNER), F32),
        pltpu.VMEM((GDN_HEADS, GDN_HEAD_DIM, GDN_HEAD_DIM), F32),
        pltpu.VMEM((n_sub, CHUNK, SSD_D_INNER), F32),
        pltpu.VMEM((n_sub, CHUNK, 2 * GDN_WIDTH), BF16),
        pltpu.VMEM((n_sub, CHUNK, GDN_WIDTH), BF16),
        pltpu.VMEM((n_sub, CHUNK, GDN_WIDTH), BF16),
        pltpu.VMEM((n_sub, CHUNK, GDN_WIDTH // 2), BF16),
    ]
    outs = pl.pallas_call(
        functools.partial(_mixer_kernel, n_pad=n_pad, n_sub=n_sub),
        grid=(n_seq, nc),
        in_specs=in_specs,
        out_specs=out_specs,
        out_shape=out_shape,
        scratch_shapes=scratch,
        compiler_params=pltpu.CompilerParams(
            dimension_semantics=("arbitrary", "arbitrary"), vmem_limit_bytes=VMEM_LIMIT),
        name="mixer",
    )(p, p, p, p, p, p, p, ps, conv_a, ssm_a4, conv_b, ssm_b, *prm)
    ya, ob, cva, ssd, cvb, gdn = outs
    return ya, ob, cva, ssd.reshape(n_seq, SSD_HEADS, SSD_HEAD_DIM, SSD_STATE), cvb, gdn


def _outproj_kernel(x_ref, ya_ref, ob_ref, ga_ref, gb_ref, wa_ref, wb_ref, nw_ref, h_ref, n_ref):
    ma = jnp.dot(ya_ref[...], wa_ref[...], preferred_element_type=F32)
    mb = jnp.dot(ob_ref[...], wb_ref[...], preferred_element_type=F32)
    h = x_ref[...] + _sigmoid(ga_ref[...]) * ma + _sigmoid(gb_ref[...]) * mb
    h_ref[...] = h
    ms = jnp.mean(h * h, axis=-1, keepdims=True)
    n_ref[...] = ((h * lax.rsqrt(ms + EPS)) * nw_ref[...]).astype(BF16)


def _outproj(x2d, ya, ob, p, w_out_a, w_out_b, norm_w, tm):
    m = x2d.shape[0]
    const = lambda i: (0, 0)
    return pl.pallas_call(
        _outproj_kernel,
        grid=(m // tm,),
        in_specs=[
            pl.BlockSpec((tm, D_MODEL), lambda i: (i, 0)),
            pl.BlockSpec((tm, SSD_D_INNER), lambda i: (i, 0)),
            pl.BlockSpec((tm, GDN_WIDTH), lambda i: (i, 0)),
            pl.BlockSpec((tm, WIDE), lambda i: (i, BLK_GA)),
            pl.BlockSpec((tm, WIDE), lambda i: (i, BLK_GB)),
            pl.BlockSpec((SSD_D_INNER, D_MODEL), const, pipeline_mode=pl.Buffered(1)),
            pl.BlockSpec((GDN_WIDTH, D_MODEL), const, pipeline_mode=pl.Buffered(1)),
            pl.BlockSpec((1, D_MODEL), const),
        ],
        out_specs=[
            pl.BlockSpec((tm, D_MODEL), lambda i: (i, 0)),
            pl.BlockSpec((tm, D_MODEL), lambda i: (i, 0)),
        ],
        out_shape=[
            jax.ShapeDtypeStruct((m, D_MODEL), F32),
            jax.ShapeDtypeStruct((m, D_MODEL), BF16),
        ],
        compiler_params=pltpu.CompilerParams(
            dimension_semantics=("arbitrary",), vmem_limit_bytes=VMEM_LIMIT),
        name="outproj",
    )(x2d, ya, ob, p, p, w_out_a, w_out_b, norm_w)


def _mlp_kernel(n_ref, h_ref, wu_ref, wd_ref, nf_ref, y_ref, acc_s):
    j = pl.program_id(1)
    hid = jnp.dot(n_ref[...], wu_ref[...], preferred_element_type=F32)
    hid = jnp.square(jnp.maximum(hid, 0.0)).astype(BF16)
    part = jnp.dot(hid, wd_ref[...], preferred_element_type=F32)

    @pl.when(j == 0)
    def _():
        acc_s[...] = h_ref[...] + part

    @pl.when(j > 0)
    def _():
        acc_s[...] += part

    @pl.when(j == pl.num_programs(1) - 1)
    def _():
        h = acc_s[...]
        ms = jnp.mean(h * h, axis=-1, keepdims=True)
        y_ref[...] = (h * lax.rsqrt(ms + EPS)) * nf_ref[...]


def _mlp(n, h, w_up, w_down, norm_f, tm, tf=1024):
    m = n.shape[0]
    return pl.pallas_call(
        _mlp_kernel,
        grid=(m // tm, D_FF // tf),
        in_specs=[
            pl.BlockSpec((tm, D_MODEL), lambda i, j: (i, 0)),
            pl.BlockSpec((tm, D_MODEL), lambda i, j: (i, 0)),
            pl.BlockSpec((D_MODEL, tf), lambda i, j: (0, j)),
            pl.BlockSpec((tf, D_MODEL), lambda i, j: (j, 0)),
            pl.BlockSpec((1, D_MODEL), lambda i, j: (0, 0)),
        ],
        out_specs=pl.BlockSpec((tm, D_MODEL), lambda i, j: (i, 0)),
        out_shape=jax.ShapeDtypeStruct((m, D_MODEL), F32),
        scratch_shapes=[pltpu.VMEM((tm, D_MODEL), F32)],
        compiler_params=pltpu.CompilerParams(
            dimension_semantics=("arbitrary", "arbitrary"), vmem_limit_bytes=VMEM_LIMIT),
        name="mlp",
    )(n, h, w_up, w_down, norm_f)


def _row_tile(m, cap):
    t = min(m, cap)
    while m % t:
        t //= 2
    return t


def kernel(x_prompt, x_sample, state_ssd_conv, state_ssd, state_gdn_conv, state_gdn, meta_tokens, norm_mix_w, w_in, ssd_conv_w, ssd_conv_b, ssd_dt_bias, ssd_a_log, ssd_d, ssd_norm_w, gdn_conv_w, gdn_dt_bias, gdn_a_log, gdn_norm_w, w_out, norm_mlp_w, w_up, w_down, norm_f_w):
    bp, sp_len, _ = x_prompt.shape
    bs, ss_len, _ = x_sample.shape
    assert w_in.shape[0] == 1, "single layer"
    assert sp_len % CHUNK == 0 and ss_len % CHUNK == 0 and N_META <= CHUNK

    wi = w_in[0]
    z_a, xbc_a, dt_a, qkv_b, z_b, beta_b, a_b, gate_a, gate_b = jnp.split(
        wi, [2048, 5120, 5152, 11296, 13344, 13360, 13376, 15424], axis=1)
    xs_w, bc_w = xbc_a[:, :SSD_D_INNER], xbc_a[:, SSD_D_INNER:]
    q_w, k_w, v_w = jnp.split(qkv_b, 3, axis=1)
    w_main = jnp.concatenate([z_a, xs_w, q_w, k_w, v_w, z_b, gate_a, gate_b, bc_w], axis=1).astype(BF16)
    zc = lambda n: jnp.zeros((D_MODEL, n), wi.dtype)
    w_small = jnp.concatenate([dt_a, a_b, zc(LANES - 48), zc(32), beta_b, zc(LANES - 48)], axis=1).astype(BF16)
    w_out_a = w_out[0, :SSD_D_INNER].astype(BF16)
    w_out_b = w_out[0, SSD_D_INNER:].astype(BF16)
    w_up_b = w_up[0].astype(BF16)
    w_down_b = w_down[0].astype(BF16)

    zrow = lambda n: jnp.zeros((1, n), F32)
    sbias = jnp.concatenate([ssd_dt_bias[0][None], gdn_dt_bias[0][None], zrow(LANES - 48)], axis=1)
    alog = jnp.concatenate([ssd_a_log[0][None], gdn_a_log[0][None], zrow(LANES - 48)], axis=1)
    dexp = jnp.repeat(ssd_d[0], SSD_HEAD_DIM)[None]
    prm = (ssd_conv_w[0], ssd_conv_b[0][None], gdn_conv_w[0], sbias, alog, dexp,
           ssd_norm_w[0][None], gdn_norm_w[0][None])
    nmix = norm_mix_w[0][None]
    nmlp = norm_mlp_w[0][None]
    nf = norm_f_w[None]

    def front(x2d, n_seq, seq_len, n_pad, conv_a, ssm_a, conv_b, ssm_b):
        p, ps = _inproj(x2d, nmix, w_main, w_small, _row_tile(x2d.shape[0], 1024))
        return p, _mixer(p, ps, conv_a, ssm_a, conv_b, ssm_b, prm, n_seq, seq_len, n_pad)

    def back(x2d, p, ya, ob):
        h, n = _outproj(x2d, ya, ob, p, w_out_a, w_out_b, nmlp, _row_tile(x2d.shape[0], 256))
        return _mlp(n, h, w_up_b, w_down_b, nf, _row_tile(x2d.shape[0], 512))

    dtp = x_prompt.dtype
    n_pad = CHUNK - N_META
    x_meta = jnp.concatenate([jnp.zeros((n_pad, D_MODEL), dtp), meta_tokens.astype(dtp)], axis=0)
    _, (_, _, cva_m, ssd_m, cvb_m, gdn_m) = front(
        x_meta, 1, CHUNK, n_pad,
        jnp.zeros((1, CONV_K - 1, SSD_CONV_CH), dtp), jnp.zeros((1, SSD_HEADS, SSD_HEAD_DIM, SSD_STATE), dtp),
        jnp.zeros((1, CONV_K - 1, GDN_CONV_CH), dtp), jnp.zeros((1, GDN_HEADS, GDN_HEAD_DIM, GDN_HEAD_DIM), dtp))

    xp2 = x_prompt.reshape(bp * sp_len, D_MODEL)
    p_p, (ya_p, ob_p, cva_p, ssd_p, cvb_p, gdn_p) = front(xp2, bp, sp_len, 0, cva_m, ssd_m, cvb_m, gdn_m)
    y_prompt = back(xp2, p_p, ya_p, ob_p).reshape(bp, sp_len, D_MODEL)

    xs2 = x_sample.reshape(bs * ss_len, D_MODEL)
    p_s, (ya_s, ob_s, cva_s, ssd_s, cvb_s, gdn_s) = front(
        xs2, bs, ss_len, 0, state_ssd_conv[0], state_ssd[0], state_gdn_conv[0], state_gdn[0])
    y_sample = back(xs2, p_s, ya_s, ob_s).reshape(bs, ss_len, D_MODEL)

    return (y_prompt, y_sample, cva_p[None], ssd_p[None], cvb_p[None], gdn_p[None],
            cva_s[None], ssd_s[None], cvb_s[None], gdn_s[None])
```

```python
import functools

import jax
import jax.numpy as jnp
from jax import lax
from jax.experimental import pallas as pl
from jax.experimental.pallas import tpu as pltpu

F32 = jnp.float32
BF16 = jnp.bfloat16
HIGHEST = lax.Precision.HIGHEST

D_MODEL = 2048
CHUNK = 64
N_META = 16
CONV_K = 4
EPS = 1e-6
LOG2E = 1.4426950408889634

SSD_HEADS = 32
SSD_HEAD_DIM = 64
SSD_GROUPS = 4
SSD_STATE = 128
SSD_D_INNER = SSD_HEADS * SSD_HEAD_DIM
SSD_BC = 2 * SSD_GROUPS * SSD_STATE
SSD_CONV_CH = SSD_D_INNER + SSD_BC
SSD_PAIRS = SSD_HEADS // 2

GDN_HEADS = 16
GDN_HEAD_DIM = 128
GDN_WIDTH = GDN_HEADS * GDN_HEAD_DIM
GDN_CONV_CH = 3 * GDN_WIDTH
D_FF = 4 * D_MODEL

LANES = 128
WIDE = 2048
BLK_ZA, BLK_XS, BLK_Q, BLK_K, BLK_V, BLK_ZB, BLK_GA, BLK_GB = range(8)
P_COLS = 8 * WIDE + SSD_BC
BLK_BC = (8 * WIDE) // SSD_BC
SMALL_COLS = 2 * LANES

VMEM_LIMIT = 56 * 1024 * 1024
CHUNKS_PER_STEP = 1


def _silu(x):
    h = 0.5 * x
    return h * jnp.tanh(h) + h


def _sigmoid(x):
    return 1.0 / (1.0 + jnp.exp(-x))


def _softplus(x):
    return jnp.maximum(x, 0.0) + jnp.log1p(jnp.exp(-jnp.abs(x)))


def _bdot(a, b):
    return jnp.dot(a.astype(BF16), b.astype(BF16), preferred_element_type=F32)


def _bdot_nt(a, b):
    return lax.dot_general(a.astype(BF16), b.astype(BF16), (((1,), (1,)), ((), ())), preferred_element_type=F32)


def _bdot_tn(a, b):
    return lax.dot_general(a.astype(BF16), b.astype(BF16), (((0,), (0,)), ((), ())), preferred_element_type=F32)


def _inproj_kernel(x_ref, nw_ref, w_ref, ws_ref, p_ref, ps_ref, u_s):
    j = pl.program_id(1)

    @pl.when(j == 0)
    def _():
        x = x_ref[...]
        ms = jnp.mean(x * x, axis=-1, keepdims=True)
        u = (x * lax.rsqrt(ms + EPS)) * nw_ref[...]
        u_s[...] = u.astype(BF16)
        ps_ref[...] = jnp.dot(u_s[...], ws_ref[...], preferred_element_type=F32)

    p_ref[...] = jnp.dot(u_s[...], w_ref[...], preferred_element_type=F32)


def _inproj(x2d, norm_w, w_main, w_small, tm, tn=1024):
    m = x2d.shape[0]
    grid = (m // tm, P_COLS // tn)
    return pl.pallas_call(
        _inproj_kernel,
        grid=grid,
        in_specs=[
            pl.BlockSpec((tm, D_MODEL), lambda i, j: (i, 0)),
            pl.BlockSpec((1, D_MODEL), lambda i, j: (0, 0)),
            pl.BlockSpec((D_MODEL, tn), lambda i, j: (0, j)),
            pl.BlockSpec((D_MODEL, SMALL_COLS), lambda i, j: (0, 0)),
        ],
        out_specs=[
            pl.BlockSpec((tm, tn), lambda i, j: (i, j)),
            pl.BlockSpec((tm, SMALL_COLS), lambda i, j: (i, 0)),
        ],
        out_shape=[
            jax.ShapeDtypeStruct((m, P_COLS), F32),
            jax.ShapeDtypeStruct((m, SMALL_COLS), F32),
        ],
        scratch_shapes=[pltpu.VMEM((tm, D_MODEL), BF16)],
        compiler_params=pltpu.CompilerParams(
            dimension_semantics=("arbitrary", "arbitrary"), vmem_limit_bytes=VMEM_LIMIT),
        name="inproj",
    )(x2d, norm_w, w_main, w_small)


def _mixer_kernel(za_ref, xs_ref, q_ref, k_ref, v_ref, zb_ref, bc_ref, sm_ref,
                  cva_ref, ssd_ref, cvb_ref, gdn_ref,
                  cwa_ref, cba_ref, cwb_ref, sbias_ref, alog_ref, dexp_ref, nwa_ref, nwb_ref,
                  ya_ref, ob_ref, cva_o, ssd_o, cvb_o, gdn_o,
                  ha, hb, xs_s, bc_s, q_s, k_s, v_s, ht_s, s_s, y_s, rhs_s, qe_s, kdec_s, qkd_s, *, n_pad, n_sub):
    c = pl.program_id(1)
    nc = pl.num_programs(1)
    T = CHUNK
    HIST = CONV_K - 1
    rows_blk = n_sub * T

    @pl.when(c == 0)
    def _():
        ha[...] = jnp.concatenate([jnp.zeros((8 - HIST, SSD_CONV_CH), F32), cva_ref[...]], axis=0)
        hb[...] = jnp.concatenate([jnp.zeros((8 - HIST, GDN_CONV_CH), F32), cvb_ref[...]], axis=0)
        for j in range(SSD_PAIRS):
            ht_s[:, j * LANES:(j + 1) * LANES] = ssd_ref[j].T
        s_s[...] = gdn_ref[...]

    @pl.when(c == nc - 1)
    def _():
        cva_o[:, 0:SSD_D_INNER] = xs_ref[rows_blk - HIST:rows_blk, :]
        cva_o[:, SSD_D_INNER:SSD_CONV_CH] = bc_ref[rows_blk - HIST:rows_blk, :]
        cvb_o[:, 0:GDN_WIDTH] = q_ref[rows_blk - HIST:rows_blk, :]
        cvb_o[:, GDN_WIDTH:2 * GDN_WIDTH] = k_ref[rows_blk - HIST:rows_blk, :]
        cvb_o[:, 2 * GDN_WIDTH:3 * GDN_WIDTH] = v_ref[rows_blk - HIST:rows_blk, :]

    lane = lax.broadcasted_iota(jnp.int32, (T, LANES), 1)
    row = lax.broadcasted_iota(jnp.int32, (T, LANES), 0)
    tr = lax.broadcasted_iota(jnp.int32, (T, T), 0)
    tc = lax.broadcasted_iota(jnp.int32, (T, T), 1)
    tril = jnp.where(tr >= tc, 1.0, 0.0).astype(F32)
    zpad = jnp.zeros((T, LANES), F32)
    lo_half = lane < T
    scol = lane & (T - 1)
    incl2 = row >= scol
    strict2 = row > scol
    mask_lo = jnp.where(lo_half, 1.0, 0.0).astype(BF16)
    mask_hi = jnp.where(lo_half, 0.0, 1.0).astype(BF16)
    eye2 = jnp.where(row == scol, 1.0, 0.0).astype(F32)
    blk16 = (row // 16) == (scol // 16)
    blk32 = (row // 32) == (scol // 32)
    zb = jnp.zeros((T, LANES), BF16)
    zb2 = jnp.zeros((T, 2 * LANES), BF16)
    q_scale = GDN_HEAD_DIM ** -0.5
    npair = GDN_HEADS // 2
    gw = SSD_D_INNER // SSD_GROUPS
    ppg = SSD_PAIRS // SSD_GROUPS

    def lane_bcast(m, col):
        return jnp.broadcast_to(m[:, col:col + 1], (T, LANES))

    def pair_cols(mt, r0, r1):
        return jnp.concatenate([mt[r0:r0 + 1, 0:T], mt[r1:r1 + 1, 0:T]], axis=1)

    def blockdiag(yp):
        yb = yp.astype(BF16)
        return jnp.concatenate([yb * mask_lo, yb * mask_hi], axis=0)

    fillers = []

    def chunk_program(u):
        rs = slice(u * T, (u + 1) * T)
        xs_u, bc_u, q_u, k_u, v_u, y_u = xs_s.at[u], bc_s.at[u], q_s.at[u], k_s.at[u], v_s.at[u], y_s.at[u]
        rhs_u, qe_u, kdec_u, qkd_u = rhs_s.at[u], qe_s.at[u], kdec_s.at[u], qkd_s.at[u]

        def conv_slab(in_ref, i_lo, hist, w_ref, b_ref, c_lo, out_ref, o_lo, wd=512):
            ci = slice(i_lo, i_lo + wd)
            cc = slice(c_lo, c_lo + wd)
            x = in_ref[rs, ci]
            full = jnp.concatenate([hist[0:8, cc], x], axis=0)
            w = [w_ref[j:j + 1, cc] for j in range(CONV_K)]
            f1 = pltpu.roll(full, 1, axis=0)
            pair_b = full * w[1] + f1 * w[0]
            y = x * w[3] + f1[8:8 + T] * w[2] + pltpu.roll(pair_b, 2, axis=0)[8:8 + T]
            if b_ref is not None:
                y = y + b_ref[:, cc]
            out_ref[:, o_lo:o_lo + wd] = _silu(y)
            hist[0:8, cc] = x[T - 8:T]

        def conv_tasks(in_ref, hist, w_ref, b_ref, c_base, out_ref, width, wd=512):
            return [functools.partial(conv_slab, in_ref, s * wd, hist, w_ref, b_ref, c_base + s * wd, out_ref, s * wd, wd)
                    for s in range(width // wd)]

        sp = _softplus(sm_ref[rs, 0:LANES] + sbias_ref[...])
        sg = _sigmoid(sm_ref[rs, LANES:2 * LANES])
        neg_a = -jnp.exp(alog_ref[...])
        used = lane < SSD_HEADS + GDN_HEADS
        gmat = jnp.where(used, sp * neg_a, 0.0)
        vmat = jnp.where(lane < SSD_HEADS, sp, jnp.where(used, sg, 0.0))
        if n_pad:
            live = row >= n_pad
            gmat = jnp.where(live, gmat, 0.0)
            vmat = jnp.where(live, vmat, 0.0)
        cum = jnp.dot(tril, gmat, precision=HIGHEST, preferred_element_type=F32) * LOG2E
        cumt = jnp.concatenate([cum, zpad], axis=0).T
        vmt = jnp.concatenate([vmat, zpad], axis=0).T
        last_all = cum[T - 1:T, :]
        wsm = jnp.exp2(last_all - cum) * vmat

        def ssd_group_tasks(g):
            st = {}
            gs = slice(g * gw, (g + 1) * gw)

            def prologue():
                bm = bc_u[:, g * SSD_STATE:(g + 1) * SSD_STATE]
                cm = bc_u[:, (SSD_GROUPS + g) * SSD_STATE:(SSD_GROUPS + g + 1) * SSD_STATE]
                st["bmb"] = bm.astype(BF16)
                st["cb2"] = _bdot_nt(cm, jnp.concatenate([st["bmb"], st["bmb"]], axis=0))
                st["yoff"] = _bdot(cm, ht_s[:, gs])
                st["xw"], st["elast"] = [], []

            def pair(jj):
                j = g * ppg + jj
                r0, r1 = 2 * j, 2 * j + 1
                sl = slice(j * LANES, (j + 1) * LANES)
                rowp = jnp.where(lo_half, lane_bcast(cum, r0), lane_bcast(cum, r1))
                wb = jnp.where(lo_half, lane_bcast(wsm, r0), lane_bcast(wsm, r1))
                colp = pair_cols(cumt, r0, r1)
                dtrow = pair_cols(vmt, r0, r1)
                lmat = jnp.exp2(jnp.where(incl2, rowp - colp, -jnp.inf))
                mp = st["cb2"] * lmat * dtrow
                xsp = xs_u[:, sl]
                e1 = jnp.exp2(rowp)
                y = (_bdot(mp, blockdiag(xsp)) + st["yoff"][:, jj * LANES:(jj + 1) * LANES] * e1
                     + xsp * dexp_ref[:, sl])
                y_u[:, sl] = y * _silu(za_ref[rs, sl])
                st["xw"].append((xsp * wb).astype(BF16))
                st["elast"].append(e1[T - 1:T, :])

            def epilogue():
                xw = jnp.concatenate(st["xw"], axis=1)
                elast = jnp.concatenate(st["elast"], axis=1)
                ht_s[:, gs] = ht_s[:, gs] * elast + _bdot_tn(st["bmb"], xw)
                yg = y_u[:, gs]
                ms = jnp.mean(yg * yg, axis=-1, keepdims=True)
                ya_ref[rs, gs] = ((yg * lax.rsqrt(ms + EPS)) * nwa_ref[:, gs]).astype(BF16)

            return [prologue] + [functools.partial(pair, jj) for jj in range(ppg)] + [epilogue]

        fillers.extend(conv_tasks(xs_ref, ha, cwa_ref, cba_ref, 0, xs_u, SSD_D_INNER)
                       + conv_tasks(bc_ref, ha, cwa_ref, cba_ref, SSD_D_INNER, bc_u, SSD_BC))
        for g in range(SSD_GROUPS):
            fillers.extend(ssd_group_tasks(g))

        for t in (conv_tasks(q_ref, hb, cwb_ref, None, 0, q_u, GDN_WIDTH)
                  + conv_tasks(k_ref, hb, cwb_ref, None, GDN_WIDTH, k_u, GDN_WIDTH)
                  + conv_tasks(v_ref, hb, cwb_ref, None, 2 * GDN_WIDTH, v_u, GDN_WIDTH)):
            t()

        qkp, decayp, ap, eg_last = [], [], [], []
        for i in range(npair):
            qn, kn, gcb, betab = [], [], [], []
            for h in (2 * i, 2 * i + 1):
                sl = slice(h * LANES, (h + 1) * LANES)
                qh = q_u[:, sl]
                kh = k_u[:, sl]
                qn.append(qh * (lax.rsqrt(jnp.sum(qh * qh, axis=-1, keepdims=True) + EPS) * q_scale))
                kn.append(kh * lax.rsqrt(jnp.sum(kh * kh, axis=-1, keepdims=True) + EPS))
                gcb.append(lane_bcast(cum, SSD_HEADS + h))
                betab.append(lane_bcast(vmat, SSD_HEADS + h))
                eg = jnp.exp2(gcb[-1])
                eg_last.append(eg[T - 1:T, :])
                rhs_u[:, 2 * h * LANES:(2 * h + 1) * LANES] = (v_u[:, sl] * betab[-1]).astype(BF16)
                rhs_u[:, (2 * h + 1) * LANES:(2 * h + 2) * LANES] = (kn[-1] * (betab[-1] * eg)).astype(BF16)
                qe_u[:, sl] = (qn[-1] * eg).astype(BF16)
                kdec_u[:, sl] = (kn[-1] * jnp.exp2(gcb[-1][T - 1:T, :] - gcb[-1])).astype(BF16)
            growp = jnp.where(lo_half, gcb[0], gcb[1])
            betap = jnp.where(lo_half, betab[0], betab[1])
            gcolp = pair_cols(cumt, SSD_HEADS + 2 * i, SSD_HEADS + 2 * i + 1)
            decayp.append(jnp.exp2(jnp.where(incl2, growp - gcolp, -jnp.inf)))
            knb = [kn[0].astype(BF16), kn[1].astype(BF16)]
            lhs = jnp.concatenate([jnp.concatenate([qn[0].astype(BF16), qn[1].astype(BF16)], axis=1),
                                   jnp.concatenate(knb, axis=1)], axis=0)
            rhst = jnp.concatenate([jnp.concatenate([knb[0], zb], axis=1),
                                    jnp.concatenate([zb, knb[1]], axis=1)], axis=0)
            qkkk = _bdot_nt(lhs, rhst)
            qkp.append(qkkk[0:T])
            ap.append(jnp.where(strict2, betap * qkkk[T:2 * T] * decayp[i], 0.0))
        yield

        for i in range(npair):
            qkd_u[:, i * LANES:(i + 1) * LANES] = (qkp[i] * decayp[i]).astype(BF16)
        d = [jnp.where(blk16, a, 0.0) for a in ap]
        pw = [_bdot(di, blockdiag(di)) for di in d]
        x = [eye2 - di for di in d]
        yield
        for lvl in range(3):
            bdp = [blockdiag(p_) for p_ in pw]
            if lvl < 2:
                r = [_bdot(jnp.concatenate([x[i], pw[i]], axis=0), bdp[i]) for i in range(npair)]
                x = [x[i] + r[i][0:T] for i in range(npair)]
                pw = [r[i][T:2 * T] for i in range(npair)]
            else:
                x = [x[i] + _bdot(x[i], bdp[i]) for i in range(npair)]
            yield
        for lvl in range(2):
            if lvl == 0:
                off = [jnp.where(blk32 & ~blk16, a, 0.0) for a in ap]
            else:
                off = [jnp.where(blk32, 0.0, a) for a in ap]
            y = [_bdot(x[i], blockdiag(off[i])) for i in range(npair)]
            yield
            x = [x[i] - _bdot(y[i], blockdiag(x[i])) for i in range(npair)]
            yield

        uw = []
        for i in range(npair):
            r0 = rhs_u[:, 4 * i * LANES:(4 * i + 2) * LANES]
            r1 = rhs_u[:, (4 * i + 2) * LANES:(4 * i + 4) * LANES]
            bdr = jnp.concatenate([jnp.concatenate([r0, zb2], axis=1),
                                   jnp.concatenate([zb2, r1], axis=1)], axis=0)
            uw.append(_bdot(x[i], bdr))
        yield

        wv, qs = [], []
        for h in range(GDN_HEADS):
            i, e = divmod(h, 2)
            u_ = uw[i][:, e * 2 * LANES:e * 2 * LANES + LANES]
            w = uw[i][:, e * 2 * LANES + LANES:(e + 1) * 2 * LANES]
            wq = _bdot(jnp.concatenate([w.astype(BF16), qe_u[:, h * LANES:(h + 1) * LANES]], axis=0), s_s[h])
            wv.append((u_ - wq[0:T]).astype(BF16))
            qs.append(wq[T:2 * T])
        yield

        oo = []
        for i in range(npair):
            bdw = jnp.concatenate([jnp.concatenate([wv[2 * i], zb], axis=1),
                                   jnp.concatenate([zb, wv[2 * i + 1]], axis=1)], axis=0)
            oo.append(_bdot(qkd_u[:, i * LANES:(i + 1) * LANES], bdw))
        yield
        for h in range(GDN_HEADS):
            i, e = divmod(h, 2)
            sl = slice(h * LANES, (h + 1) * LANES)
            oh = qs[h] + oo[i][:, e * LANES:(e + 1) * LANES]
            s_s[h] = s_s[h] * eg_last[h] + _bdot_tn(kdec_u[:, sl], wv[h])
            ms = jnp.mean(oh * oh, axis=-1, keepdims=True)
            on = (oh * lax.rsqrt(ms + EPS)) * nwb_ref[...]
            ob_ref[rs, sl] = (on * _silu(zb_ref[rs, sl])).astype(BF16)

    LAG = 6
    progs = [chunk_program(u) for u in range(n_sub)]
    started = 0
    active = []
    wave = 0
    n_waves = 13 + LAG * (n_sub - 1)
    while started < n_sub or active:
        if started < n_sub and wave >= LAG * started:
            active.append(progs[started])
            started += 1
        for p_ in list(active):
            try:
                next(p_)
            except StopIteration:
                active.remove(p_)
        per_wave = -(-len(fillers) // max(n_waves - wave, 1))
        for _ in range(min(per_wave, len(fillers))):
            fillers.pop(0)()
        wave += 1
    while fillers:
        fillers.pop(0)()

    @pl.when(c == nc - 1)
    def _():
        for j in range(SSD_PAIRS):
            ssd_o[j] = ht_s[:, j * LANES:(j + 1) * LANES].T
        gdn_o[...] = s_s[...]


def _mixer(p, ps, conv_a, ssm_a, conv_b, ssm_b, prm, n_seq, seq_len, n_pad):
    n_sub = CHUNKS_PER_STEP if seq_len % (CHUNKS_PER_STEP * CHUNK) == 0 else 1
    nc = seq_len // (n_sub * CHUNK)
    m = n_seq * seq_len
    T = n_sub * CHUNK

    def rows(blk):
        return lambda b, c: (b * nc + c, blk)

    def state_map(arr):
        shared = arr.shape[0] == 1
        nd = arr.ndim
        return lambda b, c: ((0 if shared else b),) + (0,) * (nd - 1)

    def full(arr):
        nd = arr.ndim
        return pl.BlockSpec(arr.shape, lambda b, c: (0,) * nd)

    ssm_a4 = ssm_a.reshape(ssm_a.shape[0], SSD_PAIRS, 2 * SSD_HEAD_DIM, SSD_STATE)
    in_specs = [
        pl.BlockSpec((T, WIDE), rows(BLK_ZA)),
        pl.BlockSpec((T, WIDE), rows(BLK_XS)),
        pl.BlockSpec((T, WIDE), rows(BLK_Q)),
        pl.BlockSpec((T, WIDE), rows(BLK_K)),
        pl.BlockSpec((T, WIDE), rows(BLK_V)),
        pl.BlockSpec((T, WIDE), rows(BLK_ZB)),
        pl.BlockSpec((T, SSD_BC), rows(BLK_BC)),
        pl.BlockSpec((T, SMALL_COLS), rows(0)),
        pl.BlockSpec((None, CONV_K - 1, SSD_CONV_CH), state_map(conv_a)),
        pl.BlockSpec((None, SSD_PAIRS, 2 * SSD_HEAD_DIM, SSD_STATE), state_map(ssm_a4)),
        pl.BlockSpec((None, CONV_K - 1, GDN_CONV_CH), state_map(conv_b)),
        pl.BlockSpec((None, GDN_HEADS, GDN_HEAD_DIM, GDN_HEAD_DIM), state_map(ssm_b)),
    ] + [full(a) for a in prm]
    out_specs = [
        pl.BlockSpec((T, SSD_D_INNER), lambda b, c: (b * nc + c, 0)),
        pl.BlockSpec((T, GDN_WIDTH), lambda b, c: (b * nc + c, 0)),
        pl.BlockSpec((None, CONV_K - 1, SSD_CONV_CH), lambda b, c: (b, 0, 0)),
        pl.BlockSpec((None, SSD_PAIRS, 2 * SSD_HEAD_DIM, SSD_STATE), lambda b, c: (b, 0, 0, 0)),
        pl.BlockSpec((None, CONV_K - 1, GDN_CONV_CH), lambda b, c: (b, 0, 0)),
        pl.BlockSpec((None, GDN_HEADS, GDN_HEAD_DIM, GDN_HEAD_DIM), lambda b, c: (b, 0, 0, 0)),
    ]
    out_shape = [
        jax.ShapeDtypeStruct((m, SSD_D_INNER), BF16),
        jax.ShapeDtypeStruct((m, GDN_WIDTH), BF16),
        jax.ShapeDtypeStruct((n_seq, CONV_K - 1, SSD_CONV_CH), F32),
        jax.ShapeDtypeStruct((n_seq, SSD_PAIRS, 2 * SSD_HEAD_DIM, SSD_STATE), F32),
        jax.ShapeDtypeStruct((n_seq, CONV_K - 1, GDN_CONV_CH), F32),
        jax.ShapeDtypeStruct((n_seq, GDN_HEADS, GDN_HEAD_DIM, GDN_HEAD_DIM), F32),
    ]
    scratch = [
        pltpu.VMEM((8, SSD_CONV_CH), F32),
        pltpu.VMEM((8, GDN_CONV_CH), F32),
        pltpu.VMEM((n_sub, CHUNK, SSD_D_INNER), F32),
        pltpu.VMEM((n_sub, CHUNK, SSD_BC), F32),
        pltpu.VMEM((n_sub, CHUNK, GDN_WIDTH), F32),
        pltpu.VMEM((n_sub, CHUNK, GDN_WIDTH), F32),
        pltpu.VMEM((n_sub, CHUNK, GDN_WIDTH), F32),
        pltpu.VMEM((SSD_STATE, SSD_D_INNER), F32),
        pltpu.VMEM((GDN_HEADS, GDN_HEAD_DIM, GDN_HEAD_DIM), F32),
        pltpu.VMEM((n_sub, CHUNK, SSD_D_INNER), F32),
        pltpu.VMEM((n_sub, CHUNK, 2 * GDN_WIDTH), BF16),
        pltpu.VMEM((n_sub, CHUNK, GDN_WIDTH), BF16),
        pltpu.VMEM((n_sub, CHUNK, GDN_WIDTH), BF16),
        pltpu.VMEM((n_sub, CHUNK, GDN_WIDTH // 2), BF16),
    ]
    outs = pl.pallas_call(
        functools.partial(_mixer_kernel, n_pad=n_pad, n_sub=n_sub),
        grid=(n_seq, nc),
        in_specs=in_specs,
        out_specs=out_specs,
        out_shape=out_shape,
        scratch_shapes=scratch,
        compiler_params=pltpu.CompilerParams(
            dimension_semantics=("arbitrary", "arbitrary"), vmem_limit_bytes=VMEM_LIMIT),
        name="mixer",
    )(p, p, p, p, p, p, p, ps, conv_a, ssm_a4, conv_b, ssm_b, *prm)
    ya, ob, cva, ssd, cvb, gdn = outs
    return ya, ob, cva, ssd.reshape(n_seq, SSD_HEADS, SSD_HEAD_DIM, SSD_STATE), cvb, gdn


def _outproj_kernel(x_ref, ya_ref, ob_ref, ga_ref, gb_ref, wa_ref, wb_ref, nw_ref, h_ref, n_ref):
    ma = jnp.dot(ya_ref[...], wa_ref[...], preferred_element_type=F32)
    mb = jnp.dot(ob_ref[...], wb_ref[...], preferred_element_type=F32)
    h = x_ref[...] + _sigmoid(ga_ref[...]) * ma + _sigmoid(gb_ref[...]) * mb
    h_ref[...] = h
    ms = jnp.mean(h * h, axis=-1, keepdims=True)
    n_ref[...] = ((h * lax.rsqrt(ms + EPS)) * nw_ref[...]).astype(BF16)


def _outproj(x2d, ya, ob, p, w_out_a, w_out_b, norm_w, tm):
    m = x2d.shape[0]
    const = lambda i: (0, 0)
    return pl.pallas_call(
        _outproj_kernel,
        grid=(m // tm,),
        in_specs=[
            pl.BlockSpec((tm, D_MODEL), lambda i: (i, 0)),
            pl.BlockSpec((tm, SSD_D_INNER), lambda i: (i, 0)),
            pl.BlockSpec((tm, GDN_WIDTH), lambda i: (i, 0)),
            pl.BlockSpec((tm, WIDE), lambda i: (i, BLK_GA)),
            pl.BlockSpec((tm, WIDE), lambda i: (i, BLK_GB)),
            pl.BlockSpec((SSD_D_INNER, D_MODEL), const, pipeline_mode=pl.Buffered(1)),
            pl.BlockSpec((GDN_WIDTH, D_MODEL), const, pipeline_mode=pl.Buffered(1)),
            pl.BlockSpec((1, D_MODEL), const),
        ],
        out_specs=[
            pl.BlockSpec((tm, D_MODEL), lambda i: (i, 0)),
            pl.BlockSpec((tm, D_MODEL), lambda i: (i, 0)),
        ],
        out_shape=[
            jax.ShapeDtypeStruct((m, D_MODEL), F32),
            jax.ShapeDtypeStruct((m, D_MODEL), BF16),
        ],
        compiler_params=pltpu.CompilerParams(
            dimension_semantics=("arbitrary",), vmem_limit_bytes=VMEM_LIMIT),
        name="outproj",
    )(x2d, ya, ob, p, p, w_out_a, w_out_b, norm_w)


def _mlp_kernel(n_ref, h_ref, wu_ref, wd_ref, nf_ref, y_ref, acc_s):
    j = pl.program_id(1)
    hid = jnp.dot(n_ref[...], wu_ref[...], preferred_element_type=F32)
    hid = jnp.square(jnp.maximum(hid, 0.0)).astype(BF16)
    part = jnp.dot(hid, wd_ref[...], preferred_element_type=F32)

    @pl.when(j == 0)
    def _():
        acc_s[...] = h_ref[...] + part

    @pl.when(j > 0)
    def _():
        acc_s[...] += part

    @pl.when(j == pl.num_programs(1) - 1)
    def _():
        h = acc_s[...]
        ms = jnp.mean(h * h, axis=-1, keepdims=True)
        y_ref[...] = (h * lax.rsqrt(ms + EPS)) * nf_ref[...]


def _mlp(n, h, w_up, w_down, norm_f, tm, tf=1024):
    m = n.shape[0]
    return pl.pallas_call(
        _mlp_kernel,
        grid=(m // tm, D_FF // tf),
        in_specs=[
            pl.BlockSpec((tm, D_MODEL), lambda i, j: (i, 0)),
            pl.BlockSpec((tm, D_MODEL), lambda i, j: (i, 0)),
            pl.BlockSpec((D_MODEL, tf), lambda i, j: (0, j)),
            pl.BlockSpec((tf, D_MODEL), lambda i, j: (j, 0)),
            pl.BlockSpec((1, D_MODEL), lambda i, j: (0, 0)),
        ],
        out_specs=pl.BlockSpec((tm, D_MODEL), lambda i, j: (i, 0)),
        out_shape=jax.ShapeDtypeStruct((m, D_MODEL), F32),
        scratch_shapes=[pltpu.VMEM((tm, D_MODEL), F32)],
        compiler_params=pltpu.CompilerParams(
            dimension_semantics=("arbitrary", "arbitrary"), vmem_limit_bytes=VMEM_LIMIT),
        name="mlp",
    )(n, h, w_up, w_down, norm_f)


def _row_tile(m, cap):
    t = min(m, cap)
    while m % t:
        t //= 2
    return t


def kernel(x_prompt, x_sample, state_ssd_conv, state_ssd, state_gdn_conv, state_gdn, meta_tokens, norm_mix_w, w_in, ssd_conv_w, ssd_conv_b, ssd_dt_bias, ssd_a_log, ssd_d, ssd_norm_w, gdn_conv_w, gdn_dt_bias, gdn_a_log, gdn_norm_w, w_out, norm_mlp_w, w_up, w_down, norm_f_w):
    bp, sp_len, _ = x_prompt.shape
    bs, ss_len, _ = x_sample.shape
    assert w_in.shape[0] == 1, "single layer"
    assert sp_len % CHUNK == 0 and ss_len % CHUNK == 0 and N_META <= CHUNK

    wi = w_in[0]
    z_a, xbc_a, dt_a, qkv_b, z_b, beta_b, a_b, gate_a, gate_b = jnp.split(
        wi, [2048, 5120, 5152, 11296, 13344, 13360, 13376, 15424], axis=1)
    xs_w, bc_w = xbc_a[:, :SSD_D_INNER], xbc_a[:, SSD_D_INNER:]
    q_w, k_w, v_w = jnp.split(qkv_b, 3, axis=1)
    w_main = jnp.concatenate([z_a, xs_w, q_w, k_w, v_w, z_b, gate_a, gate_b, bc_w], axis=1).astype(BF16)
    zc = lambda n: jnp.zeros((D_MODEL, n), wi.dtype)
    w_small = jnp.concatenate([dt_a, a_b, zc(LANES - 48), zc(32), beta_b, zc(LANES - 48)], axis=1).astype(BF16)
    w_out_a = w_out[0, :SSD_D_INNER].astype(BF16)
    w_out_b = w_out[0, SSD_D_INNER:].astype(BF16)
    w_up_b = w_up[0].astype(BF16)
    w_down_b = w_down[0].astype(BF16)

    zrow = lambda n: jnp.zeros((1, n), F32)
    sbias = jnp.concatenate([ssd_dt_bias[0][None], gdn_dt_bias[0][None], zrow(LANES - 48)], axis=1)
    alog = jnp.concatenate([ssd_a_log[0][None], gdn_a_log[0][None], zrow(LANES - 48)], axis=1)
    dexp = jnp.repeat(ssd_d[0], SSD_HEAD_DIM)[None]
    prm = (ssd_conv_w[0], ssd_conv_b[0][None], gdn_conv_w[0], sbias, alog, dexp,
           ssd_norm_w[0][None], gdn_norm_w[0][None])
    nmix = norm_mix_w[0][None]
    nmlp = norm_mlp_w[0][None]
    nf = norm_f_w[None]

    def front(x2d, n_seq, seq_len, n_pad, conv_a, ssm_a, conv_b, ssm_b):
        p, ps = _inproj(x2d, nmix, w_main, w_small, _row_tile(x2d.shape[0], 1024))
        return p, _mixer(p, ps, conv_a, ssm_a, conv_b, ssm_b, prm, n_seq, seq_len, n_pad)

    def back(x2d, p, ya, ob):
        h, n = _outproj(x2d, ya, ob, p, w_out_a, w_out_b, nmlp, _row_tile(x2d.shape[0], 256))
        return _mlp(n, h, w_up_b, w_down_b, nf, _row_tile(x2d.shape[0], 512))

    dtp = x_prompt.dtype
    n_pad = CHUNK - N_META
    x_meta = jnp.concatenate([jnp.zeros((n_pad, D_MODEL), dtp), meta_tokens.astype(dtp)], axis=0)
    _, (_, _, cva_m, ssd_m, cvb_m, gdn_m) = front(
        x_meta, 1, CHUNK, n_pad,
        jnp.zeros((1, CONV_K - 1, SSD_CONV_CH), dtp), jnp.zeros((1, SSD_HEADS, SSD_HEAD_DIM, SSD_STATE), dtp),
        jnp.zeros((1, CONV_K - 1, GDN_CONV_CH), dtp), jnp.zeros((1, GDN_HEADS, GDN_HEAD_DIM, GDN_HEAD_DIM), dtp))

    xp2 = x_prompt.reshape(bp * sp_len, D_MODEL)
    p_p, (ya_p, ob_p, cva_p, ssd_p, cvb_p, gdn_p) = front(xp2, bp, sp_len, 0, cva_m, ssd_m, cvb_m, gdn_m)
    y_prompt = back(xp2, p_p, ya_p, ob_p).reshape(bp, sp_len, D_MODEL)

    xs2 = x_sample.reshape(bs * ss_len, D_MODEL)
    p_s, (ya_s, ob_s, cva_s, ssd_s, cvb_s, gdn_s) = front(
        xs2, bs, ss_len, 0, state_ssd_conv[0], state_ssd[0], state_gdn_conv[0], state_gdn[0])
    y_sample = back(xs2, p_s, ya_s, ob_s).reshape(bs, ss_len, D_MODEL)

    return (y_prompt, y_sample, cva_p[None], ssd_p[None], cvb_p[None], gdn_p[None],
            cva_s[None], ssd_s[None], cvb_s[None], gdn_s[None])
```

```python
import functools

import jax
import jax.numpy as jnp
from jax import lax
from jax.experimental import pallas as pl
from jax.experimental.pallas import tpu as pltpu

F32 = jnp.float32
BF16 = jnp.bfloat16
HIGHEST = lax.Precision.HIGHEST

D_MODEL = 2048
CHUNK = 64
N_META = 16
CONV_K = 4
EPS = 1e-6
LOG2E = 1.4426950408889634

SSD_HEADS = 32
SSD_HEAD_DIM = 64
SSD_GROUPS = 4
SSD_STATE = 128
SSD_D_INNER = SSD_HEADS * SSD_HEAD_DIM
SSD_BC = 2 * SSD_GROUPS * SSD_STATE
SSD_CONV_CH = SSD_D_INNER + SSD_BC
SSD_PAIRS = SSD_HEADS // 2

GDN_HEADS = 16
GDN_HEAD_DIM = 128
GDN_WIDTH = GDN_HEADS * GDN_HEAD_DIM
GDN_CONV_CH = 3 * GDN_WIDTH
D_FF = 4 * D_MODEL

LANES = 128
WIDE = 2048
BLK_Q, BLK_K, BLK_V, BLK_ZB, BLK_GA, BLK_GB, BLK_ZA, BLK_XS = range(8)
P_COLS = 8 * WIDE + SSD_BC
BLK_BC = (8 * WIDE) // SSD_BC
SMALL_COLS = 2 * LANES

VMEM_LIMIT = 56 * 1024 * 1024
CHUNKS_PER_STEP = 1


def _silu(x):
    h = 0.5 * x
    return h * jnp.tanh(h) + h


def _sigmoid(x):
    return 1.0 / (1.0 + jnp.exp(-x))


def _softplus(x):
    return jnp.maximum(x, 0.0) + jnp.log1p(jnp.exp(-jnp.abs(x)))


def _bdot(a, b):
    return jnp.dot(a.astype(BF16), b.astype(BF16), preferred_element_type=F32)


def _bdot_nt(a, b):
    return lax.dot_general(a.astype(BF16), b.astype(BF16), (((1,), (1,)), ((), ())), preferred_element_type=F32)


def _bdot_tn(a, b):
    return lax.dot_general(a.astype(BF16), b.astype(BF16), (((0,), (0,)), ((), ())), preferred_element_type=F32)


def _inproj_kernel(x_ref, nw_ref, wb_ref, wa_ref, ws_ref, p_ref, ps_ref, u_s, *, nb):
    j = pl.program_id(1)

    @pl.when(j == 0)
    def _():
        x = x_ref[...]
        ms = jnp.mean(x * x, axis=-1, keepdims=True)
        u = (x * lax.rsqrt(ms + EPS)) * nw_ref[...]
        u_s[...] = u.astype(BF16)
        ps_ref[...] = jnp.dot(u_s[...], ws_ref[...], preferred_element_type=F32)

    @pl.when(j < nb)
    def _():
        p_ref[...] = jnp.dot(u_s[...], wb_ref[...], preferred_element_type=F32)

    @pl.when(j >= nb)
    def _():
        p_ref[...] = jnp.dot(u_s[...], wa_ref[...], preferred_element_type=F32)


def _inproj(x2d, norm_w, w_b, w_a, w_small, tm, tn=1024):
    m = x2d.shape[0]
    nb, na = w_b.shape[1] // tn, w_a.shape[1] // tn
    assert (nb + na) * tn == P_COLS
    return pl.pallas_call(
        functools.partial(_inproj_kernel, nb=nb),
        grid=(m // tm, nb + na),
        in_specs=[
            pl.BlockSpec((tm, D_MODEL), lambda i, j: (i, 0)),
            pl.BlockSpec((1, D_MODEL), lambda i, j: (0, 0)),
            pl.BlockSpec((D_MODEL, tn), lambda i, j: (0, jnp.minimum(j, nb - 1))),
            pl.BlockSpec((D_MODEL, tn), lambda i, j: (0, jnp.maximum(j - nb, 0))),
            pl.BlockSpec((D_MODEL, SMALL_COLS), lambda i, j: (0, 0)),
        ],
        out_specs=[
            pl.BlockSpec((tm, tn), lambda i, j: (i, j)),
            pl.BlockSpec((tm, SMALL_COLS), lambda i, j: (i, 0)),
        ],
        out_shape=[
            jax.ShapeDtypeStruct((m, P_COLS), F32),
            jax.ShapeDtypeStruct((m, SMALL_COLS), F32),
        ],
        scratch_shapes=[pltpu.VMEM((tm, D_MODEL), BF16)],
        compiler_params=pltpu.CompilerParams(
            dimension_semantics=("arbitrary", "arbitrary"), vmem_limit_bytes=VMEM_LIMIT),
        name="inproj",
    )(x2d, norm_w, w_b, w_a, w_small)


def _mixer_kernel(za_ref, xs_ref, q_ref, k_ref, v_ref, zb_ref, bc_ref, sm_ref,
                  cva_ref, ssd_ref, cvb_ref, gdn_ref,
                  cwa_ref, cba_ref, cwb_ref, sbias_ref, alog_ref, dexp_ref, nwa_ref, nwb_ref,
                  ya_ref, ob_ref, cva_o, ssd_o, cvb_o, gdn_o,
                  ha, hb, xs_s, bc_s, q_s, k_s, v_s, ht_s, s_s, y_s, rhs_s, qe_s, kdec_s, qkd_s, *, n_pad, n_sub):
    c = pl.program_id(1)
    nc = pl.num_programs(1)
    T = CHUNK
    HIST = CONV_K - 1
    rows_blk = n_sub * T

    @pl.when(c == 0)
    def _():
        ha[...] = jnp.concatenate([jnp.zeros((8 - HIST, SSD_CONV_CH), F32), cva_ref[...]], axis=0)
        hb[...] = jnp.concatenate([jnp.zeros((8 - HIST, GDN_CONV_CH), F32), cvb_ref[...]], axis=0)
        for j in range(SSD_PAIRS):
            ht_s[:, j * LANES:(j + 1) * LANES] = ssd_ref[j].T
        s_s[...] = gdn_ref[...]

    @pl.when(c == nc - 1)
    def _():
        cva_o[:, 0:SSD_D_INNER] = xs_ref[rows_blk - HIST:rows_blk, :]
        cva_o[:, SSD_D_INNER:SSD_CONV_CH] = bc_ref[rows_blk - HIST:rows_blk, :]
        cvb_o[:, 0:GDN_WIDTH] = q_ref[rows_blk - HIST:rows_blk, :]
        cvb_o[:, GDN_WIDTH:2 * GDN_WIDTH] = k_ref[rows_blk - HIST:rows_blk, :]
        cvb_o[:, 2 * GDN_WIDTH:3 * GDN_WIDTH] = v_ref[rows_blk - HIST:rows_blk, :]

    lane = lax.broadcasted_iota(jnp.int32, (T, LANES), 1)
    row = lax.broadcasted_iota(jnp.int32, (T, LANES), 0)
    tr = lax.broadcasted_iota(jnp.int32, (T, T), 0)
    tc = lax.broadcasted_iota(jnp.int32, (T, T), 1)
    tril = jnp.where(tr >= tc, 1.0, 0.0).astype(F32)
    zpad = jnp.zeros((T, LANES), F32)
    lo_half = lane < T
    scol = lane & (T - 1)
    incl2 = row >= scol
    strict2 = row > scol
    mask_lo = jnp.where(lo_half, 1.0, 0.0).astype(BF16)
    mask_hi = jnp.where(lo_half, 0.0, 1.0).astype(BF16)
    eye2 = jnp.where(row == scol, 1.0, 0.0).astype(F32)
    blk16 = (row // 16) == (scol // 16)
    blk32 = (row // 32) == (scol // 32)
    zb = jnp.zeros((T, LANES), BF16)
    zb2 = jnp.zeros((T, 2 * LANES), BF16)
    q_scale = GDN_HEAD_DIM ** -0.5
    npair = GDN_HEADS // 2
    gw = SSD_D_INNER // SSD_GROUPS
    ppg = SSD_PAIRS // SSD_GROUPS

    def lane_bcast(m, col):
        return jnp.broadcast_to(m[:, col:col + 1], (T, LANES))

    def pair_cols(mt, r0, r1):
        return jnp.concatenate([mt[r0:r0 + 1, 0:T], mt[r1:r1 + 1, 0:T]], axis=1)

    def blockdiag(yp):
        yb = yp.astype(BF16)
        return jnp.concatenate([yb * mask_lo, yb * mask_hi], axis=0)

    fillers = []

    def chunk_program(u):
        rs = slice(u * T, (u + 1) * T)
        xs_u, bc_u, q_u, k_u, v_u, y_u = xs_s.at[u], bc_s.at[u], q_s.at[u], k_s.at[u], v_s.at[u], y_s.at[u]
        rhs_u, qe_u, kdec_u, qkd_u = rhs_s.at[u], qe_s.at[u], kdec_s.at[u], qkd_s.at[u]

        def conv_slab(in_ref, i_lo, hist, w_ref, b_ref, c_lo, out_ref, o_lo, wd=512):
            ci = slice(i_lo, i_lo + wd)
            cc = slice(c_lo, c_lo + wd)
            x = in_ref[rs, ci]
            full = jnp.concatenate([hist[0:8, cc], x], axis=0)
            w = [w_ref[j:j + 1, cc] for j in range(CONV_K)]
            f1 = pltpu.roll(full, 1, axis=0)
            pair_b = full * w[1] + f1 * w[0]
            y = x * w[3] + f1[8:8 + T] * w[2] + pltpu.roll(pair_b, 2, axis=0)[8:8 + T]
            if b_ref is not None:
                y = y + b_ref[:, cc]
            out_ref[:, o_lo:o_lo + wd] = _silu(y)
            hist[0:8, cc] = x[T - 8:T]

        def conv_tasks(in_ref, hist, w_ref, b_ref, c_base, out_ref, width, wd=512):
            return [functools.partial(conv_slab, in_ref, s * wd, hist, w_ref, b_ref, c_base + s * wd, out_ref, s * wd, wd)
                    for s in range(width // wd)]

        sp = _softplus(sm_ref[rs, 0:LANES] + sbias_ref[...])
        sg = _sigmoid(sm_ref[rs, LANES:2 * LANES])
        neg_a = -jnp.exp(alog_ref[...])
        used = lane < SSD_HEADS + GDN_HEADS
        gmat = jnp.where(used, sp * neg_a, 0.0)
        vmat = jnp.where(lane < SSD_HEADS, sp, jnp.where(used, sg, 0.0))
        if n_pad:
            live = row >= n_pad
            gmat = jnp.where(live, gmat, 0.0)
            vmat = jnp.where(live, vmat, 0.0)
        cum = jnp.dot(tril, gmat, precision=HIGHEST, preferred_element_type=F32) * LOG2E
        cumt = jnp.concatenate([cum, zpad], axis=0).T
        vmt = jnp.concatenate([vmat, zpad], axis=0).T
        last_all = cum[T - 1:T, :]
        wsm = jnp.exp2(last_all - cum) * vmat

        def ssd_group_tasks(g):
            st = {}
            gs = slice(g * gw, (g + 1) * gw)

            def prologue():
                bm = bc_u[:, g * SSD_STATE:(g + 1) * SSD_STATE]
                cm = bc_u[:, (SSD_GROUPS + g) * SSD_STATE:(SSD_GROUPS + g + 1) * SSD_STATE]
                st["bmb"] = bm.astype(BF16)
                st["cb2"] = _bdot_nt(cm, jnp.concatenate([st["bmb"], st["bmb"]], axis=0))
                st["yoff"] = _bdot(cm, ht_s[:, gs])
                st["xw"], st["elast"] = [], []

            def pair(jj):
                j = g * ppg + jj
                r0, r1 = 2 * j, 2 * j + 1
                sl = slice(j * LANES, (j + 1) * LANES)
                rowp = jnp.where(lo_half, lane_bcast(cum, r0), lane_bcast(cum, r1))
                wb = jnp.where(lo_half, lane_bcast(wsm, r0), lane_bcast(wsm, r1))
                colp = pair_cols(cumt, r0, r1)
                dtrow = pair_cols(vmt, r0, r1)
                lmat = jnp.exp2(jnp.where(incl2, rowp - colp, -jnp.inf))
                mp = st["cb2"] * lmat * dtrow
                xsp = xs_u[:, sl]
                e1 = jnp.exp2(rowp)
                y = (_bdot(mp, blockdiag(xsp)) + st["yoff"][:, jj * LANES:(jj + 1) * LANES] * e1
                     + xsp * dexp_ref[:, sl])
                y_u[:, sl] = y * _silu(za_ref[rs, sl])
                st["xw"].append((xsp * wb).astype(BF16))
                st["elast"].append(e1[T - 1:T, :])

            def epilogue():
                xw = jnp.concatenate(st["xw"], axis=1)
                elast = jnp.concatenate(st["elast"], axis=1)
                ht_s[:, gs] = ht_s[:, gs] * elast + _bdot_tn(st["bmb"], xw)
                yg = y_u[:, gs]
                ms = jnp.mean(yg * yg, axis=-1, keepdims=True)
                ya_ref[rs, gs] = ((yg * lax.rsqrt(ms + EPS)) * nwa_ref[:, gs]).astype(BF16)

            return [prologue] + [functools.partial(pair, jj) for jj in range(ppg)] + [epilogue]

        fillers.extend(conv_tasks(xs_ref, ha, cwa_ref, cba_ref, 0, xs_u, SSD_D_INNER)
                       + conv_tasks(bc_ref, ha, cwa_ref, cba_ref, SSD_D_INNER, bc_u, SSD_BC))
        for g in range(SSD_GROUPS):
            fillers.extend(ssd_group_tasks(g))

        for t in (conv_tasks(q_ref, hb, cwb_ref, None, 0, q_u, GDN_WIDTH)
                  + conv_tasks(k_ref, hb, cwb_ref, None, GDN_WIDTH, k_u, GDN_WIDTH)
                  + conv_tasks(v_ref, hb, cwb_ref, None, 2 * GDN_WIDTH, v_u, GDN_WIDTH)):
            t()

        qkp, decayp, ap, eg_last = [], [], [], []
        for i in range(npair):
            qn, kn, gcb, betab = [], [], [], []
            for h in (2 * i, 2 * i + 1):
                sl = slice(h * LANES, (h + 1) * LANES)
                qh = q_u[:, sl]
                kh = k_u[:, sl]
                qn.append(qh * (lax.rsqrt(jnp.sum(qh * qh, axis=-1, keepdims=True) + EPS) * q_scale))
                kn.append(kh * lax.rsqrt(jnp.sum(kh * kh, axis=-1, keepdims=True) + EPS))
                gcb.append(lane_bcast(cum, SSD_HEADS + h))
                betab.append(lane_bcast(vmat, SSD_HEADS + h))
                eg = jnp.exp2(gcb[-1])
                eg_last.append(eg[T - 1:T, :])
                rhs_u[:, 2 * h * LANES:(2 * h + 1) * LANES] = (v_u[:, sl] * betab[-1]).astype(BF16)
                rhs_u[:, (2 * h + 1) * LANES:(2 * h + 2) * LANES] = (kn[-1] * (betab[-1] * eg)).astype(BF16)
                qe_u[:, sl] = (qn[-1] * eg).astype(BF16)
                kdec_u[:, sl] = (kn[-1] * jnp.exp2(gcb[-1][T - 1:T, :] - gcb[-1])).astype(BF16)
            growp = jnp.where(lo_half, gcb[0], gcb[1])
            betap = jnp.where(lo_half, betab[0], betab[1])
            gcolp = pair_cols(cumt, SSD_HEADS + 2 * i, SSD_HEADS + 2 * i + 1)
            decayp.append(jnp.exp2(jnp.where(incl2, growp - gcolp, -jnp.inf)))
            knb = [kn[0].astype(BF16), kn[1].astype(BF16)]
            lhs = jnp.concatenate([jnp.concatenate([qn[0].astype(BF16), qn[1].astype(BF16)], axis=1),
                                   jnp.concatenate(knb, axis=1)], axis=0)
            rhst = jnp.concatenate([jnp.concatenate([knb[0], zb], axis=1),
                                    jnp.concatenate([zb, knb[1]], axis=1)], axis=0)
            qkkk = _bdot_nt(lhs, rhst)
            qkp.append(qkkk[0:T])
            ap.append(jnp.where(strict2, betap * qkkk[T:2 * T] * decayp[i], 0.0))
        yield

        for i in range(npair):
            qkd_u[:, i * LANES:(i + 1) * LANES] = (qkp[i] * decayp[i]).astype(BF16)
        d = [jnp.where(blk16, a, 0.0) for a in ap]
        pw = [_bdot(di, blockdiag(di)) for di in d]
        x = [eye2 - di for di in d]
        yield
        for lvl in range(3):
            bdp = [blockdiag(p_) for p_ in pw]
            if lvl < 2:
                r = [_bdot(jnp.concatenate([x[i], pw[i]], axis=0), bdp[i]) for i in range(npair)]
                x = [x[i] + r[i][0:T] for i in range(npair)]
                pw = [r[i][T:2 * T] for i in range(npair)]
            else:
                x = [x[i] + _bdot(x[i], bdp[i]) for i in range(npair)]
            yield
        for lvl in range(2):
            if lvl == 0:
                off = [jnp.where(blk32 & ~blk16, a, 0.0) for a in ap]
            else:
                off = [jnp.where(blk32, 0.0, a) for a in ap]
            y = [_bdot(x[i], blockdiag(off[i])) for i in range(npair)]
            yield
            x = [x[i] - _bdot(y[i], blockdiag(x[i])) for i in range(npair)]
            yield

        uw = []
        for i in range(npair):
            r0 = rhs_u[:, 4 * i * LANES:(4 * i + 2) * LANES]
            r1 = rhs_u[:, (4 * i + 2) * LANES:(4 * i + 4) * LANES]
            bdr = jnp.concatenate([jnp.concatenate([r0, zb2], axis=1),
                                   jnp.concatenate([zb2, r1], axis=1)], axis=0)
            uw.append(_bdot(x[i], bdr))
        yield

        wv, qs = [], []
        for h in range(GDN_HEADS):
            i, e = divmod(h, 2)
            u_ = uw[i][:, e * 2 * LANES:e * 2 * LANES + LANES]
            w = uw[i][:, e * 2 * LANES + LANES:(e + 1) * 2 * LANES]
            wq = _bdot(jnp.concatenate([w.astype(BF16), qe_u[:, h * LANES:(h + 1) * LANES]], axis=0), s_s[h])
            wv.append((u_ - wq[0:T]).astype(BF16))
            qs.append(wq[T:2 * T])
        yield

        oo = []
        for i in range(npair):
            bdw = jnp.concatenate([jnp.concatenate([wv[2 * i], zb], axis=1),
                                   jnp.concatenate([zb, wv[2 * i + 1]], axis=1)], axis=0)
            oo.append(_bdot(qkd_u[:, i * LANES:(i + 1) * LANES], bdw))
        yield
        for h in range(GDN_HEADS):
            i, e = divmod(h, 2)
            sl = slice(h * LANES, (h + 1) * LANES)
            oh = qs[h] + oo[i][:, e * LANES:(e + 1) * LANES]
            s_s[h] = s_s[h] * eg_last[h] + _bdot_tn(kdec_u[:, sl], wv[h])
            ms = jnp.mean(oh * oh, axis=-1, keepdims=True)
            on = (oh * lax.rsqrt(ms + EPS)) * nwb_ref[...]
            ob_ref[rs, sl] = (on * _silu(zb_ref[rs, sl])).astype(BF16)

    LAG = 6
    progs = [chunk_program(u) for u in range(n_sub)]
    started = 0
    active = []
    wave = 0
    n_waves = 13 + LAG * (n_sub - 1)
    while started < n_sub or active:
        if started < n_sub and wave >= LAG * started:
            active.append(progs[started])
            started += 1
        for p_ in list(active):
            try:
                next(p_)
            except StopIteration:
                active.remove(p_)
        per_wave = -(-len(fillers) // max(n_waves - wave, 1))
        for _ in range(min(per_wave, len(fillers))):
            fillers.pop(0)()
        wave += 1
    while fillers:
        fillers.pop(0)()

    @pl.when(c == nc - 1)
    def _():
        for j in range(SSD_PAIRS):
            ssd_o[j] = ht_s[:, j * LANES:(j + 1) * LANES].T
        gdn_o[...] = s_s[...]


def _mixer(p, ps, conv_a, ssm_a, conv_b, ssm_b, prm, n_seq, seq_len, n_pad):
    n_sub = CHUNKS_PER_STEP if seq_len % (CHUNKS_PER_STEP * CHUNK) == 0 else 1
    nc = seq_len // (n_sub * CHUNK)
    m = n_seq * seq_len
    T = n_sub * CHUNK

    def rows(blk):
        return lambda b, c: (b * nc + c, blk)

    def state_map(arr):
        shared = arr.shape[0] == 1
        nd = arr.ndim
        return lambda b, c: ((0 if shared else b),) + (0,) * (nd - 1)

    def full(arr):
        nd = arr.ndim
        return pl.BlockSpec(arr.shape, lambda b, c: (0,) * nd)

    ssm_a4 = ssm_a.reshape(ssm_a.shape[0], SSD_PAIRS, 2 * SSD_HEAD_DIM, SSD_STATE)
    in_specs = [
        pl.BlockSpec((T, WIDE), rows(BLK_ZA)),
        pl.BlockSpec((T, WIDE), rows(BLK_XS)),
        pl.BlockSpec((T, WIDE), rows(BLK_Q)),
        pl.BlockSpec((T, WIDE), rows(BLK_K)),
        pl.BlockSpec((T, WIDE), rows(BLK_V)),
        pl.BlockSpec((T, WIDE), rows(BLK_ZB)),
        pl.BlockSpec((T, SSD_BC), rows(BLK_BC)),
        pl.BlockSpec((T, SMALL_COLS), rows(0)),
        pl.BlockSpec((None, CONV_K - 1, SSD_CONV_CH), state_map(conv_a)),
        pl.BlockSpec((None, SSD_PAIRS, 2 * SSD_HEAD_DIM, SSD_STATE), state_map(ssm_a4)),
        pl.BlockSpec((None, CONV_K - 1, GDN_CONV_CH), state_map(conv_b)),
        pl.BlockSpec((None, GDN_HEADS, GDN_HEAD_DIM, GDN_HEAD_DIM), state_map(ssm_b)),
    ] + [full(a) for a in prm]
    out_specs = [
        pl.BlockSpec((T, SSD_D_INNER), lambda b, c: (b * nc + c, 0)),
        pl.BlockSpec((T, GDN_WIDTH), lambda b, c: (b * nc + c, 0)),
        pl.BlockSpec((None, CONV_K - 1, SSD_CONV_CH), lambda b, c: (b, 0, 0)),
        pl.BlockSpec((None, SSD_PAIRS, 2 * SSD_HEAD_DIM, SSD_STATE), lambda b, c: (b, 0, 0, 0)),
        pl.BlockSpec((None, CONV_K - 1, GDN_CONV_CH), lambda b, c: (b, 0, 0)),
        pl.BlockSpec((None, GDN_HEADS, GDN_HEAD_DIM, GDN_HEAD_DIM), lambda b, c: (b, 0, 0, 0)),
    ]
    out_shape = [
        jax.ShapeDtypeStruct((m, SSD_D_INNER), BF16),
        jax.ShapeDtypeStruct((m, GDN_WIDTH), BF16),
        jax.ShapeDtypeStruct((n_seq, CONV_K - 1, SSD_CONV_CH), F32),
        jax.ShapeDtypeStruct((n_seq, SSD_PAIRS, 2 * SSD_HEAD_DIM, SSD_STATE), F32),
        jax.ShapeDtypeStruct((n_seq, CONV_K - 1, GDN_CONV_CH), F32),
        jax.ShapeDtypeStruct((n_seq, GDN_HEADS, GDN_HEAD_DIM, GDN_HEAD_DIM), F32),
    ]
    scratch = [
        pltpu.VMEM((8, SSD_CONV_CH), F32),
        pltpu.VMEM((8, GDN_CONV_CH), F32),
        pltpu.VMEM((n_sub, CHUNK, SSD_D_INNER), F32),
        pltpu.VMEM((n_sub, CHUNK, SSD_BC), F32),
        pltpu.VMEM((n_sub, CHUNK, GDN_WIDTH), F32),
        pltpu.VMEM((n_sub, CHUNK, GDN_WIDTH), F32),
        pltpu.VMEM((n_sub, CHUNK, GDN_WIDTH), F32),
        pltpu.VMEM((SSD_STATE, SSD_D_INNER), F32),
        pltpu.VMEM((GDN_HEADS, GDN_HEAD_DIM, GDN_HEAD_DIM), F32),
        pltpu.VMEM((n_sub, CHUNK, SSD_D_INNER), F32),
        pltpu.VMEM((n_sub, CHUNK, 2 * GDN_WIDTH), BF16),
        pltpu.VMEM((n_sub, CHUNK, GDN_WIDTH), BF16),
        pltpu.VMEM((n_sub, CHUNK, GDN_WIDTH), BF16),
        pltpu.VMEM((n_sub, CHUNK, GDN_WIDTH // 2), BF16),
    ]
    outs = pl.pallas_call(
        functools.partial(_mixer_kernel, n_pad=n_pad, n_sub=n_sub),
        grid=(n_seq, nc),
        in_specs=in_specs,
        out_specs=out_specs,
        out_shape=out_shape,
        scratch_shapes=scratch,
        compiler_params=pltpu.CompilerParams(
            dimension_semantics=("arbitrary", "arbitrary"), vmem_limit_bytes=VMEM_LIMIT),
        name="mixer",
    )(p, p, p, p, p, p, p, ps, conv_a, ssm_a4, conv_b, ssm_b, *prm)
    ya, ob, cva, ssd, cvb, gdn = outs
    return ya, ob, cva, ssd.reshape(n_seq, SSD_HEADS, SSD_HEAD_DIM, SSD_STATE), cvb, gdn


def _outproj_kernel(x_ref, ya_ref, ob_ref, ga_ref, gb_ref, wa_ref, wb_ref, nw_ref, h_ref, n_ref):
    ma = jnp.dot(ya_ref[...], wa_ref[...], preferred_element_type=F32)
    mb = jnp.dot(ob_ref[...], wb_ref[...], preferred_element_type=F32)
    h = x_ref[...] + _sigmoid(ga_ref[...]) * ma + _sigmoid(gb_ref[...]) * mb
    h_ref[...] = h
    ms = jnp.mean(h * h, axis=-1, keepdims=True)
    n_ref[...] = ((h * lax.rsqrt(ms + EPS)) * nw_ref[...]).astype(BF16)


def _outproj(x2d, ya, ob, p, w_out_a, w_out_b, norm_w, tm):
    m = x2d.shape[0]
    const = lambda i: (0, 0)
    return pl.pallas_call(
        _outproj_kernel,
        grid=(m // tm,),
        in_specs=[
            pl.BlockSpec((tm, D_MODEL), lambda i: (i, 0)),
            pl.BlockSpec((tm, SSD_D_INNER), lambda i: (i, 0)),
            pl.BlockSpec((tm, GDN_WIDTH), lambda i: (i, 0)),
            pl.BlockSpec((tm, WIDE), lambda i: (i, BLK_GA)),
            pl.BlockSpec((tm, WIDE), lambda i: (i, BLK_GB)),
            pl.BlockSpec((SSD_D_INNER, D_MODEL), const, pipeline_mode=pl.Buffered(1)),
            pl.BlockSpec((GDN_WIDTH, D_MODEL), const, pipeline_mode=pl.Buffered(1)),
            pl.BlockSpec((1, D_MODEL), const),
        ],
        out_specs=[
            pl.BlockSpec((tm, D_MODEL), lambda i: (i, 0)),
            pl.BlockSpec((tm, D_MODEL), lambda i: (i, 0)),
        ],
        out_shape=[
            jax.ShapeDtypeStruct((m, D_MODEL), F32),
            jax.ShapeDtypeStruct((m, D_MODEL), BF16),
        ],
        compiler_params=pltpu.CompilerParams(
            dimension_semantics=("arbitrary",), vmem_limit_bytes=VMEM_LIMIT),
        name="outproj",
    )(x2d, ya, ob, p, p, w_out_a, w_out_b, norm_w)


def _mlp_kernel(n_ref, h_ref, wu_ref, wd_ref, nf_ref, y_ref, acc_s):
    j = pl.program_id(1)
    hid = jnp.dot(n_ref[...], wu_ref[...], preferred_element_type=F32)
    hid = jnp.square(jnp.maximum(hid, 0.0)).astype(BF16)
    part = jnp.dot(hid, wd_ref[...], preferred_element_type=F32)

    @pl.when(j == 0)
    def _():
        acc_s[...] = h_ref[...] + part

    @pl.when(j > 0)
    def _():
        acc_s[...] += part

    @pl.when(j == pl.num_programs(1) - 1)
    def _():
        h = acc_s[...]
        ms = jnp.mean(h * h, axis=-1, keepdims=True)
        y_ref[...] = (h * lax.rsqrt(ms + EPS)) * nf_ref[...]


def _mlp(n, h, w_up, w_down, norm_f, tm, tf=1024):
    m = n.shape[0]
    return pl.pallas_call(
        _mlp_kernel,
        grid=(m // tm, D_FF // tf),
        in_specs=[
            pl.BlockSpec((tm, D_MODEL), lambda i, j: (i, 0)),
            pl.BlockSpec((tm, D_MODEL), lambda i, j: (i, 0)),
            pl.BlockSpec((D_MODEL, tf), lambda i, j: (0, j)),
            pl.BlockSpec((tf, D_MODEL), lambda i, j: (j, 0)),
            pl.BlockSpec((1, D_MODEL), lambda i, j: (0, 0)),
        ],
        out_specs=pl.BlockSpec((tm, D_MODEL), lambda i, j: (i, 0)),
        out_shape=jax.ShapeDtypeStruct((m, D_MODEL), F32),
        scratch_shapes=[pltpu.VMEM((tm, D_MODEL), F32)],
        compiler_params=pltpu.CompilerParams(
            dimension_semantics=("arbitrary", "arbitrary"), vmem_limit_bytes=VMEM_LIMIT),
        name="mlp",
    )(n, h, w_up, w_down, norm_f)


def _row_tile(m, cap):
    t = min(m, cap)
    while m % t:
        t //= 2
    return t


def kernel(x_prompt, x_sample, state_ssd_conv, state_ssd, state_gdn_conv, state_gdn, meta_tokens, norm_mix_w, w_in, ssd_conv_w, ssd_conv_b, ssd_dt_bias, ssd_a_log, ssd_d, ssd_norm_w, gdn_conv_w, gdn_dt_bias, gdn_a_log, gdn_norm_w, w_out, norm_mlp_w, w_up, w_down, norm_f_w):
    bp, sp_len, _ = x_prompt.shape
    bs, ss_len, _ = x_sample.shape
    assert w_in.shape[0] == 1, "single layer"
    assert sp_len % CHUNK == 0 and ss_len % CHUNK == 0 and N_META <= CHUNK

    wi = w_in[0]
    c_dt, c_qkv, c_beta, c_ab, c_ga = 5120, 5152, 13344, 13360, 13376
    assert wi.shape[1] == c_ga + 2 * D_MODEL
    w_b = wi[:, c_qkv:c_beta].astype(BF16)
    w_a = jnp.concatenate([wi[:, c_ga:], wi[:, :c_dt]], axis=1).astype(BF16)
    zc = lambda n: jnp.zeros((D_MODEL, n), wi.dtype)
    w_small = jnp.concatenate([wi[:, c_dt:c_qkv], wi[:, c_ab:c_ga], zc(LANES - 48),
                               zc(32), wi[:, c_beta:c_ab], zc(LANES - 48)], axis=1).astype(BF16)
    w_out_a = w_out[0, :SSD_D_INNER].astype(BF16)
    w_out_b = w_out[0, SSD_D_INNER:].astype(BF16)
    w_up_b = w_up[0].astype(BF16)
    w_down_b = w_down[0].astype(BF16)

    zrow = lambda n: jnp.zeros((1, n), F32)
    sbias = jnp.concatenate([ssd_dt_bias[0][None], gdn_dt_bias[0][None], zrow(LANES - 48)], axis=1)
    alog = jnp.concatenate([ssd_a_log[0][None], gdn_a_log[0][None], zrow(LANES - 48)], axis=1)
    dexp = jnp.repeat(ssd_d[0], SSD_HEAD_DIM)[None]
    prm = (ssd_conv_w[0], ssd_conv_b[0][None], gdn_conv_w[0], sbias, alog, dexp,
           ssd_norm_w[0][None], gdn_norm_w[0][None])
    nmix = norm_mix_w[0][None]
    nmlp = norm_mlp_w[0][None]
    nf = norm_f_w[None]

    def front(x2d, n_seq, seq_len, n_pad, conv_a, ssm_a, conv_b, ssm_b):
        p, ps = _inproj(x2d, nmix, w_b, w_a, w_small, _row_tile(x2d.shape[0], 1024))
        return p, _mixer(p, ps, conv_a, ssm_a, conv_b, ssm_b, prm, n_seq, seq_len, n_pad)

    def back(x2d, p, ya, ob):
        h, n = _outproj(x2d, ya, ob, p, w_out_a, w_out_b, nmlp, _row_tile(x2d.shape[0], 256))
        return _mlp(n, h, w_up_b, w_down_b, nf, _row_tile(x2d.shape[0], 512))

    dtp = x_prompt.dtype
    n_pad = CHUNK - N_META
    x_meta = jnp.concatenate([jnp.zeros((n_pad, D_MODEL), dtp), meta_tokens.astype(dtp)], axis=0)
    _, (_, _, cva_m, ssd_m, cvb_m, gdn_m) = front(
        x_meta, 1, CHUNK, n_pad,
        jnp.zeros((1, CONV_K - 1, SSD_CONV_CH), dtp), jnp.zeros((1, SSD_HEADS, SSD_HEAD_DIM, SSD_STATE), dtp),
        jnp.zeros((1, CONV_K - 1, GDN_CONV_CH), dtp), jnp.zeros((1, GDN_HEADS, GDN_HEAD_DIM, GDN_HEAD_DIM), dtp))

    xp2 = x_prompt.reshape(bp * sp_len, D_MODEL)
    p_p, (ya_p, ob_p, cva_p, ssd_p, cvb_p, gdn_p) = front(xp2, bp, sp_len, 0, cva_m, ssd_m, cvb_m, gdn_m)
    y_prompt = back(xp2, p_p, ya_p, ob_p).reshape(bp, sp_len, D_MODEL)

    xs2 = x_sample.reshape(bs * ss_len, D_MODEL)
    p_s, (ya_s, ob_s, cva_s, ssd_s, cvb_s, gdn_s) = front(
        xs2, bs, ss_len, 0, state_ssd_conv[0], state_ssd[0], state_gdn_conv[0], state_gdn[0])
    y_sample = back(xs2, p_s, ya_s, ob_s).reshape(bs, ss_len, D_MODEL)

    return (y_prompt, y_sample, cva_p[None], ssd_p[None], cvb_p[None], gdn_p[None],
            cva_s[None], ssd_s[None], cvb_s[None], gdn_s[None])
```
